```python
import jax, jax.numpy as jnp
from jax import lax
import numpy as np

D_MODEL = 1024
BATCH = 8
SEQ = 8192
DEPTH = 1
DEC_BATCH = 128
DEC_SEQ = 8
PAST_LEN = 8192
PAGE_SIZE = 128

WINDOWS = (128, 512, 2048)
DILATIONS = (1, 4, 16)
N_GROUPS = 3
HEADS_PER_GROUP = 4
HEAD_DIM = D_MODEL // 16
GROUP_WIDTH = HEADS_PER_GROUP * HEAD_DIM
ATTN_WIDTH = N_GROUPS * GROUP_WIDTH
CONV_CH = D_MODEL - ATTN_WIDTH
MIX_WIDTH = ATTN_WIDTH + CONV_CH
CONV_K = 3
IN_WIDTH = 3 * ATTN_WIDTH + 3 * CONV_CH
SPLITS = (ATTN_WIDTH, 2 * ATTN_WIDTH, 3 * ATTN_WIDTH,
          3 * ATTN_WIDTH + CONV_CH, 3 * ATTN_WIDTH + 2 * CONV_CH)
D_FF = 4 * D_MODEL
QBLOCK = 128
EPS = 1e-6

kernel_name = "hybrid_dilated_swa_shortconv_decode_step"


def rmsnorm(x, g):
    xf = x.astype(jnp.float32)
    y = xf * lax.rsqrt(jnp.mean(xf * xf, axis=-1, keepdims=True) + EPS)
    return (y * g.astype(jnp.float32)).astype(x.dtype)


def dilated_group_attn(q, k_ctx, v_ctx, q_idx, dilation, n_keys):
    idx = q_idx[:, None] - dilation * jnp.arange(n_keys, dtype=jnp.int32)[None, :]
    valid = idx >= 0
    idx = jnp.maximum(idx, 0)
    k_g = jnp.take(k_ctx, idx, axis=1)
    v_g = jnp.take(v_ctx, idx, axis=1)
    s = jnp.einsum("bthd,btkhd->bhtk", q, k_g).astype(jnp.float32) * (HEAD_DIM ** -0.5)
    s = jnp.where(valid[None, None], s, -jnp.inf)
    lse = jax.nn.logsumexp(s, axis=-1)
    p = jnp.exp(s - lse[..., None])
    o = jnp.einsum("bhtk,btkhd->bthd", p.astype(v_ctx.dtype), v_g)
    return o, jnp.transpose(lse, (0, 2, 1))


def attend_groups(q, k_ctxs, v_ctxs, q_idxs):
    b, t = q.shape[0], q.shape[1]
    outs, lses = [], []
    for g in range(N_GROUPS):
        o, lse = dilated_group_attn(q[:, :, g], k_ctxs[g], v_ctxs[g], q_idxs[g],
                                    DILATIONS[g], WINDOWS[g] // DILATIONS[g] + 1)
        outs.append(o)
        lses.append(lse)
    o = jnp.stack(outs, axis=2)
    lse = jnp.stack(lses, axis=2)
    alpha = jax.nn.softmax(lse, axis=2)
    return (o * alpha[..., None].astype(o.dtype)).reshape(b, t, ATTN_WIDTH)


def mixing_sublayer(xn, kv_pasts, conv_past, w_in, conv_w, w_out, blocked):
    b, t, _ = xn.shape
    proj = jnp.einsum("btd,de->bte", xn, w_in)
    q, k, v, gate_b, gate_c, h = jnp.split(proj, SPLITS, axis=-1)
    q = q.reshape(b, t, N_GROUPS, HEADS_PER_GROUP, HEAD_DIM)
    k = k.reshape(b, t, N_GROUPS, HEADS_PER_GROUP, HEAD_DIM)
    v = v.reshape(b, t, N_GROUPS, HEADS_PER_GROUP, HEAD_DIM)

    k_ctxs, v_ctxs, q_idxs, new_kv = [], [], [], []
    for g in range(N_GROUPS):
        kv_new = jnp.stack([k[:, :, g], v[:, :, g]], axis=2)
        if kv_pasts is None:
            ctx = kv_new
            base = 0
        else:
            ctx = jnp.concatenate([kv_pasts[g], kv_new], axis=1)
            base = kv_pasts[g].shape[1]
        n_ctx = ctx.shape[1]
        k_ctxs.append(ctx[:, :, 0])
        v_ctxs.append(ctx[:, :, 1])
        q_idxs.append(base + jnp.arange(t, dtype=jnp.int32))
        new_kv.append(ctx[:, n_ctx - min(WINDOWS[g], n_ctx):])

    if blocked:
        def block_fn(start):
            qb = lax.dynamic_slice_in_dim(q, start, QBLOCK, axis=1)
            idx = start + jnp.arange(QBLOCK, dtype=jnp.int32)
            return attend_groups(qb, k_ctxs, v_ctxs, [idx] * N_GROUPS)
        starts = jnp.arange(t // QBLOCK, dtype=jnp.int32) * QBLOCK
        attn = lax.map(block_fn, starts)
        attn = jnp.transpose(attn, (1, 0, 2, 3)).reshape(b, t, ATTN_WIDTH)
    else:
        attn = attend_groups(q, k_ctxs, v_ctxs, q_idxs)

    u = gate_c * h
    past = jnp.zeros((b, CONV_K - 1, CONV_CH), u.dtype) if conv_past is None else conv_past
    cctx = jnp.concatenate([past, u], axis=1)
    y = conv_w[0] * cctx[:, 0:t]
    for j in range(1, CONV_K):
        y = y + conv_w[j] * cctx[:, j:j + t]
    conv_out = gate_b * y
    new_conv = cctx[:, cctx.shape[1] - (CONV_K - 1):]

    mixed = jnp.concatenate([attn, conv_out], axis=-1)
    return jnp.einsum("bte,ed->btd", mixed, w_out), new_kv, new_conv


def squared_relu_mlp(x, w_up, w_down):
    hdn = jnp.square(jax.nn.relu(jnp.einsum("btd,df->btf", x, w_up)))
    return jnp.einsum("btf,fd->btd", hdn, w_down)


def run_trunk(x, kv_caches, conv_state, norm_attn_g, w_in, conv_w, w_out,
              norm_mlp_g, w_up, w_down, norm_final_g, blocked):
    new_kv = [[], [], []]
    new_conv = []
    for l in range(DEPTH):
        kv_pasts = None if kv_caches is None else [c[l] for c in kv_caches]
        cpast = None if conv_state is None else conv_state[l]
        mix, kv_l, conv_l = mixing_sublayer(rmsnorm(x, norm_attn_g[l]), kv_pasts, cpast,
                                            w_in[l], conv_w[l], w_out[l], blocked)
        x = x + mix
        x = x + squared_relu_mlp(rmsnorm(x, norm_mlp_g[l]), w_up[l], w_down[l])
        for g in range(N_GROUPS):
            new_kv[g].append(kv_l[g])
        new_conv.append(conv_l)
    y = rmsnorm(x, norm_final_g)
    return (y, jnp.stack(new_kv[0]), jnp.stack(new_kv[1]), jnp.stack(new_kv[2]), jnp.stack(new_conv))


def setup_inputs(seed: int = 0) -> dict:
    key = jax.random.key(seed)
    ks = jax.random.split(key, 16)
    f32 = jnp.float32
    lens = [min(w, PAST_LEN) for w in WINDOWS]
    return {
        "x_prompt": jax.random.normal(ks[0], (BATCH, SEQ, D_MODEL), f32),
        "x_sample": jax.random.normal(ks[1], (DEC_BATCH, DEC_SEQ, D_MODEL), f32),
        "cache_kv_w128": jax.random.normal(ks[2], (DEPTH, DEC_BATCH, lens[0], 2, HEADS_PER_GROUP, HEAD_DIM), f32),
        "cache_kv_w512": jax.random.normal(ks[3], (DEPTH, DEC_BATCH, lens[1], 2, HEADS_PER_GROUP, HEAD_DIM), f32),
        "cache_kv_w2048": jax.random.normal(ks[4], (DEPTH, DEC_BATCH, lens[2], 2, HEADS_PER_GROUP, HEAD_DIM), f32),
        "state_conv": 0.5 * jax.random.normal(ks[5], (DEPTH, DEC_BATCH, CONV_K - 1, CONV_CH), f32),
        "norm_attn_g": 1.0 + 0.02 * jax.random.normal(ks[6], (DEPTH, D_MODEL), f32),
        "w_in": jax.random.normal(ks[7], (DEPTH, D_MODEL, IN_WIDTH), f32) * D_MODEL ** -0.5,
        "conv_w": jax.random.normal(ks[8], (DEPTH, CONV_K, CONV_CH), f32) * CONV_K ** -0.5,
        "w_out": jax.random.normal(ks[9], (DEPTH, MIX_WIDTH, D_MODEL), f32) * MIX_WIDTH ** -0.5,
        "norm_mlp_g": 1.0 + 0.02 * jax.random.normal(ks[10], (DEPTH, D_MODEL), f32),
        "w_up": jax.random.normal(ks[11], (DEPTH, D_MODEL, D_FF), f32) * D_MODEL ** -0.5,
        "w_down": jax.random.normal(ks[12], (DEPTH, D_FF, D_MODEL), f32) * D_FF ** -0.5,
        "norm_final_g": 1.0 + 0.02 * jax.random.normal(ks[13], (D_MODEL,), f32),
    }


def reference(x_prompt, x_sample, cache_kv_w128, cache_kv_w512, cache_kv_w2048, state_conv,
              norm_attn_g, w_in, conv_w, w_out, norm_mlp_g, w_up, w_down, norm_final_g):
    y_prompt, kv128_p, kv512_p, kv2048_p, conv_p = run_trunk(
        x_prompt, None, None, norm_attn_g, w_in, conv_w, w_out,
        norm_mlp_g, w_up, w_down, norm_final_g, True)
    y_sample, kv128_s, kv512_s, kv2048_s, conv_s = run_trunk(
        x_sample, [cache_kv_w128, cache_kv_w512, cache_kv_w2048], state_conv,
        norm_attn_g, w_in, conv_w, w_out, norm_mlp_g, w_up, w_down, norm_final_g, False)
    return (y_prompt, y_sample, kv128_p, kv512_p, kv2048_p, conv_p,
            kv128_s, kv512_s, kv2048_s, conv_s)
```

```python
import functools

import jax
import jax.numpy as jnp
from jax import lax
from jax.experimental import pallas as pl
from jax.experimental.pallas import tpu as pltpu

F32 = jnp.float32
BF16 = jnp.bfloat16

EPS = 1e-6
N_GROUPS = 3
HEADS = 4
HEAD_DIM = 64
GROUP_W = HEADS * HEAD_DIM
ATTN_W = N_GROUPS * GROUP_W
CONV_CH = 256
CONV_K = 3
KV_W = 2 * GROUP_W
WINDOWS = (128, 512, 2048)
DILATIONS = (1, 4, 16)
SPAN = 128
QK_SCALE = HEAD_DIM ** -0.5

_Q0, _K0, _V0, _B0, _C0, _H0, _END = 0, 768, 1536, 2304, 2560, 2816, 3072

SUBLANES = 8
LANES = 128
VMEM_LIMIT = 56 * 1024 * 1024

PROMPT_TM = 512
ATTN_QB = 512
TILE = 128


def _rmsnorm(x, g):
    y = x * lax.rsqrt(jnp.mean(x * x, axis=-1, keepdims=True) + EPS)
    return y * g


def _head_mask(h, dtype):
    lane = lax.broadcasted_iota(jnp.int32, (1, GROUP_W), 1)
    return ((lane >= h * HEAD_DIM) & (lane < (h + 1) * HEAD_DIM)).astype(F32).astype(dtype)


def _proj_prompt_kernel(x_ref, g_ref, w_ref, cw_ref,
                        q_ref, k_ref, v_ref, co_ref, kv0_ref, kv1_ref, kv2_ref, ct_ref,
                        carry_ref, *, tm):
    i = pl.program_id(1)

    @pl.when(i == 0)
    def _():
        carry_ref[...] = jnp.zeros_like(carry_ref)

    xn = _rmsnorm(x_ref[0], g_ref[...]).astype(BF16)

    def proj(lo, hi):
        return jnp.dot(xn, w_ref[:, lo:hi], preferred_element_type=F32)

    q_ref[0] = (proj(_Q0, _K0) * QK_SCALE).astype(BF16)
    kf = proj(_K0, _V0)
    vf = proj(_V0, _B0)
    k_ref[0] = kf.astype(BF16)
    v_ref[0] = vf.astype(BF16)
    for g, kv_ref in enumerate((kv0_ref, kv1_ref, kv2_ref)):
        rows = kv_ref.shape[1]
        kv_ref[0, :, 0:GROUP_W] = kf[tm - rows:tm, g * GROUP_W:(g + 1) * GROUP_W]
        kv_ref[0, :, GROUP_W:KV_W] = vf[tm - rows:tm, g * GROUP_W:(g + 1) * GROUP_W]

    gate_b = proj(_B0, _C0)
    u = proj(_C0, _H0) * proj(_H0, _END)
    row = lax.broadcasted_iota(jnp.int32, (tm, CONV_CH), 0)
    carry = carry_ref[...]
    prev1 = jnp.where(row == 0, carry[7:8, :], pltpu.roll(u, 1, 0))
    prev2 = jnp.where(row == 0, carry[6:7, :],
                      jnp.where(row == 1, carry[7:8, :], pltpu.roll(u, 2, 0)))
    y = cw_ref[0:1, :] * prev2
    y = y + cw_ref[1:2, :] * prev1
    y = y + cw_ref[2:3, :] * u
    co_ref[0] = (gate_b * y).astype(BF16)
    tail = u[tm - SUBLANES:tm, :]
    carry_ref[...] = tail
    ct_ref[0] = tail


def _proj_prompt(x, g, w_in, conv_w):
    b, t, d = x.shape
    tm = PROMPT_TM
    nt = t // tm
    assert t % tm == 0 and t >= WINDOWS[-1]

    def tail_spec(w):
        rows = min(w, tm)
        nblk = max(w // tm, 1)
        return (pl.BlockSpec((1, rows, KV_W), lambda bi, i: (bi, jnp.maximum(i - (nt - nblk), 0), 0)),
                jax.ShapeDtypeStruct((b, w, KV_W), F32))

    tails = [tail_spec(w) for w in WINDOWS]
    tok = lambda width: pl.BlockSpec((1, tm, width), lambda bi, i: (bi, i, 0))
    const = lambda shape: pl.BlockSpec(shape, lambda bi, i: (0, 0))
    return pl.pallas_call(
        functools.partial(_proj_prompt_kernel, tm=tm),
        grid=(b, nt),
        in_specs=[tok(d), const((1, d)), const((d, _END)), const((CONV_K, CONV_CH))],
        out_specs=[tok(ATTN_W), tok(ATTN_W), tok(ATTN_W), tok(CONV_CH)] + [s for s, _ in tails]
                  + [pl.BlockSpec((1, SUBLANES, CONV_CH), lambda bi, i: (bi, 0, 0))],
        out_shape=[jax.ShapeDtypeStruct((b, t, ATTN_W), BF16)] * 3
                  + [jax.ShapeDtypeStruct((b, t, CONV_CH), BF16)] + [s for _, s in tails]
                  + [jax.ShapeDtypeStruct((b, SUBLANES, CONV_CH), F32)],
        scratch_shapes=[pltpu.VMEM((SUBLANES, CONV_CH), F32)],
        compiler_params=pltpu.CompilerParams(
            dimension_semantics=("arbitrary", "arbitrary"), vmem_limit_bytes=VMEM_LIMIT),
        name="proj_prompt",
    )(x, g, w_in, conv_w)


def _proj_sample_kernel(x_ref, g_ref, w_ref, cw_ref, sp_ref,
                        q_ref, kn_ref, vn_ref, co_ref, u_ref, *, tm):
    xn = _rmsnorm(x_ref[...], g_ref[...]).astype(BF16)

    def proj(lo, hi):
        return jnp.dot(xn, w_ref[:, lo:hi], preferred_element_type=F32)

    q_ref[...] = proj(_Q0, _K0) * QK_SCALE
    kn_ref[...] = proj(_K0, _V0)
    vn_ref[...] = proj(_V0, _B0)
    gate_b = proj(_B0, _C0)
    u = proj(_C0, _H0) * proj(_H0, _END)
    step = lax.broadcasted_iota(jnp.int32, (tm, CONV_CH), 0) & (SUBLANES - 1)
    sp = sp_ref[...]
    prev1 = jnp.where(step == 0, pltpu.roll(sp, tm - 1, 0), pltpu.roll(u, 1, 0))
    prev2 = jnp.where(step < 2, sp, pltpu.roll(u, 2, 0))
    y = cw_ref[0:1, :] * prev2
    y = y + cw_ref[1:2, :] * prev1
    y = y + cw_ref[2:3, :] * u
    co_ref[...] = (gate_b * y).astype(BF16)
    u_ref[...] = u


def _proj_sample(x, g, w_in, conv_w, state_pad):
    n, d = x.shape
    tm = min(n, PROMPT_TM)
    assert n % tm == 0 and tm % SUBLANES == 0
    tok = lambda width: pl.BlockSpec((tm, width), lambda i: (i, 0))
    const = lambda shape: pl.BlockSpec(shape, lambda i: (0, 0))
    return pl.pallas_call(
        functools.partial(_proj_sample_kernel, tm=tm),
        grid=(n // tm,),
        in_specs=[tok(d), const((1, d)), const((d, _END)), const((CONV_K, CONV_CH)), tok(CONV_CH)],
        out_specs=[tok(ATTN_W), tok(ATTN_W), tok(ATTN_W), tok(CONV_CH), tok(CONV_CH)],
        out_shape=[jax.ShapeDtypeStruct((n, ATTN_W), F32)] * 3
                  + [jax.ShapeDtypeStruct((n, CONV_CH), BF16), jax.ShapeDtypeStruct((n, CONV_CH), F32)],
        compiler_params=pltpu.CompilerParams(
            dimension_semantics=("arbitrary",), vmem_limit_bytes=VMEM_LIMIT),
        name="proj_sample",
    )(x, g, w_in, conv_w, state_pad)


def _attn_prompt_kernel(q_ref, k_ref, v_ref, o_ref, l_ref, kext_ref, vext_ref, *, qb):
    ub = pl.program_id(2)

    @pl.when(ub == 0)
    def _():
        kext_ref[0:SPAN, :] = jnp.zeros((SPAN, GROUP_W), BF16)
        vext_ref[0:SPAN, :] = jnp.zeros((SPAN, GROUP_W), BF16)

    kext_ref[SPAN:SPAN + qb, :] = k_ref[0]
    vext_ref[SPAN:SPAN + qb, :] = v_ref[0]

    row = lax.broadcasted_iota(jnp.int32, (TILE, TILE + SPAN), 0)
    col = lax.broadcasted_iota(jnp.int32, (TILE, TILE + SPAN), 1)
    band = (col >= row) & (col <= row + SPAN)
    first_off = jnp.where(ub == 0, 0, SPAN)
    band_first = band & (col + first_off >= SPAN)
    lane128 = lax.broadcasted_iota(jnp.int32, (TILE, LANES), 1)
    lane256 = lax.broadcasted_iota(jnp.int32, (TILE, GROUP_W), 1)

    for j in range(qb // TILE):
        q = q_ref[0, j * TILE:(j + 1) * TILE, :]
        kk = kext_ref[j * TILE:j * TILE + TILE + SPAN, :]
        vv = vext_ref[j * TILE:j * TILE + TILE + SPAN, :]
        valid = band_first if j == 0 else band
        o_acc = jnp.zeros((TILE, GROUP_W), F32)
        l_acc = jnp.zeros((TILE, LANES), F32)
        for h in range(HEADS):
            qh = q * _head_mask(h, BF16)
            s = lax.dot_general(qh, kk, (((1,), (1,)), ((), ())), preferred_element_type=F32)
            s = jnp.where(valid, s, -jnp.inf)
            m = jnp.max(s, axis=-1, keepdims=True)
            p = jnp.exp(s - m)
            z = jnp.sum(p, axis=-1, keepdims=True)
            of = jnp.dot(p.astype(BF16), vv, preferred_element_type=F32)
            in_head = (lane256 >= h * HEAD_DIM) & (lane256 < (h + 1) * HEAD_DIM)
            o_acc = jnp.where(in_head, of / z, o_acc)
            l_acc = jnp.where(lane128 == h, m + jnp.log(z), l_acc)
        o_ref[0, j * TILE:(j + 1) * TILE, :] = o_acc.astype(BF16)
        l_ref[0, j * TILE:(j + 1) * TILE, :] = l_acc

    kext_ref[0:SPAN, :] = k_ref[0, qb - SPAN:qb, :]
    vext_ref[0:SPAN, :] = v_ref[0, qb - SPAN:qb, :]


def _attn_prompt(q, k, v, g):
    b, t, _ = q.shape
    d = DILATIONS[g]
    tc = t // d
    qb = min(ATTN_QB, tc)
    assert t % d == 0 and tc % qb == 0 and qb % TILE == 0
    view = lambda a: a.reshape(b, tc, d * ATTN_W)
    in_spec = pl.BlockSpec((1, qb, GROUP_W), lambda bi, r, ub: (bi, ub, r * N_GROUPS + g))
    o, lse = pl.pallas_call(
        functools.partial(_attn_prompt_kernel, qb=qb),
        grid=(b, d, tc // qb),
        in_specs=[in_spec, in_spec, in_spec],
        out_specs=[pl.BlockSpec((1, qb, GROUP_W), lambda bi, r, ub: (bi, ub, r)),
                   pl.BlockSpec((1, qb, LANES), lambda bi, r, ub: (bi, ub, r))],
        out_shape=[jax.ShapeDtypeStruct((b, tc, d * GROUP_W), BF16),
                   jax.ShapeDtypeStruct((b, tc, d * LANES), F32)],
        scratch_shapes=[pltpu.VMEM((SPAN + qb, GROUP_W), BF16), pltpu.VMEM((SPAN + qb, GROUP_W), BF16)],
        compiler_params=pltpu.CompilerParams(
            dimension_semantics=("arbitrary", "arbitrary", "arbitrary"), vmem_limit_bytes=VMEM_LIMIT),
        name=f"attn_prompt_g{g}",
    )(view(q), view(k), view(v))
    return o.reshape(b * t, GROUP_W), lse.reshape(b * t, LANES)


def _attn_sample_kernel(q_ref, kn_ref, vn_ref, c0_ref, c1_ref, c2_ref,
                        o_ref, n0_ref, n1_ref, n2_ref, *, steps):
    rows = HEADS * steps
    step_of = lambda shape: lax.broadcasted_iota(jnp.int32, shape, 0) & (steps - 1)
    lane256 = lax.broadcasted_iota(jnp.int32, (steps, GROUP_W), 1)
    pad = jnp.zeros((LANES - steps, GROUP_W), F32)
    outs, lses = [], []
    for g, (c_ref, n_ref) in enumerate(((c0_ref, n0_ref), (c1_ref, n1_ref), (c2_ref, n2_ref))):
        w = c_ref.shape[1]
        d = DILATIONS[g]
        kn = kn_ref[0, :, g * GROUP_W:(g + 1) * GROUP_W]
        vn = vn_ref[0, :, g * GROUP_W:(g + 1) * GROUP_W]
        n_ref[0, 0:w - steps, :] = c_ref[0, steps:w, :]
        n_ref[0, w - steps:w, 0:GROUP_W] = kn
        n_ref[0, w - steps:w, GROUP_W:KV_W] = vn

        qg = q_ref[0, :, g * GROUP_W:(g + 1) * GROUP_W]
        qs = jnp.concatenate([qg * _head_mask(h, F32) for h in range(HEADS)], axis=0).astype(BF16)
        ck = c_ref[0, :, 0:GROUP_W].astype(BF16)
        cv = c_ref[0, :, GROUP_W:KV_W].astype(BF16)
        knp = jnp.concatenate([kn, pad], axis=0).astype(BF16)
        vnp = jnp.concatenate([vn, pad], axis=0).astype(BF16)
        nt = (((1,), (1,)), ((), ()))
        s_c = lax.dot_general(qs, ck, nt, preferred_element_type=F32)
        s_n = lax.dot_general(qs, knp, nt, preferred_element_type=F32)
        t_c = step_of((rows, w))
        i_c = lax.broadcasted_iota(jnp.int32, (rows, w), 1)
        valid_c = (i_c >= t_c) & (((i_c - t_c) & (d - 1)) == 0)
        t_n = step_of((rows, LANES))
        i_n = lax.broadcasted_iota(jnp.int32, (rows, LANES), 1)
        valid_n = (i_n <= t_n) & (((t_n - i_n) & (d - 1)) == 0)
        s_c = jnp.where(valid_c, s_c, -jnp.inf)
        s_n = jnp.where(valid_n, s_n, -jnp.inf)
        m = jnp.maximum(jnp.max(s_c, axis=-1, keepdims=True), jnp.max(s_n, axis=-1, keepdims=True))
        p_c = jnp.exp(s_c - m)
        p_n = jnp.exp(s_n - m)
        z = jnp.sum(p_c, axis=-1, keepdims=True) + jnp.sum(p_n, axis=-1, keepdims=True)
        of = jnp.dot(p_c.astype(BF16), cv, preferred_element_type=F32)
        of = of + jnp.dot(p_n.astype(BF16), vnp, preferred_element_type=F32)
        outs.append(of / z)
        lses.append(m + jnp.log(z))

    top = jnp.maximum(jnp.maximum(lses[0], lses[1]), lses[2])
    es = [jnp.exp(l - top) for l in lses]
    den = es[0] + es[1] + es[2]
    for g in range(N_GROUPS):
        og = outs[g] * (es[g] / den)
        res = jnp.zeros((steps, GROUP_W), F32)
        for h in range(HEADS):
            in_head = (lane256 >= h * HEAD_DIM) & (lane256 < (h + 1) * HEAD_DIM)
            res = jnp.where(in_head, og[h * steps:(h + 1) * steps, :], res)
        o_ref[0, :, g * GROUP_W:(g + 1) * GROUP_W] = res


def _attn_sample(q, kn, vn, caches):
    nb, steps, _ = q.shape
    assert steps == SUBLANES and all(c.shape[1] == w for c, w in zip(caches, WINDOWS))
    tok = pl.BlockSpec((1, steps, ATTN_W), lambda i: (i, 0, 0))
    cspecs = [pl.BlockSpec((1, w, KV_W), lambda i: (i, 0, 0)) for w in WINDOWS]
    return pl.pallas_call(
        functools.partial(_attn_sample_kernel, steps=steps),
        grid=(nb,),
        in_specs=[tok, tok, tok] + cspecs,
        out_specs=[tok] + cspecs,
        out_shape=[jax.ShapeDtypeStruct((nb, steps, ATTN_W), F32)]
                  + [jax.ShapeDtypeStruct((nb, w, KV_W), F32) for w in WINDOWS],
        compiler_params=pltpu.CompilerParams(
            dimension_semantics=("arbitrary",), vmem_limit_bytes=VMEM_LIMIT),
        name="attn_sample",
    )(q, kn, vn, *caches)


def _post_kernel(*refs, tm, combine):
    if combine:
        (x_ref, o0_ref, o1_ref, o2_ref, l0_ref, l1_ref, l2_ref, co_ref,
         wo_ref, g2_ref, wu_ref, wd_ref, gf_ref, y_ref) = refs
        ls = [l0_ref[...], l1_ref[...], l2_ref[...]]
        top = jnp.maximum(jnp.maximum(ls[0], ls[1]), ls[2])
        es = [jnp.exp(l - top) for l in ls]
        den = es[0] + es[1] + es[2]
        lane = lax.broadcasted_iota(jnp.int32, (tm, GROUP_W), 1)
        parts = []
        for o_ref, e in zip((o0_ref, o1_ref, o2_ref), es):
            alpha = e / den
            wide = jnp.zeros((tm, GROUP_W), F32)
            for h in range(HEADS):
                in_head = (lane >= h * HEAD_DIM) & (lane < (h + 1) * HEAD_DIM)
                wide = jnp.where(in_head, alpha[:, h:h + 1], wide)
            parts.append((o_ref[...].astype(F32) * wide).astype(BF16))
        mixed = jnp.concatenate(parts + [co_ref[...]], axis=-1)
    else:
        x_ref, o_ref, co_ref, wo_ref, g2_ref, wu_ref, wd_ref, gf_ref, y_ref = refs
        mixed = jnp.concatenate([o_ref[...].astype(BF16), co_ref[...]], axis=-1)

    x1 = x_ref[...] + jnp.dot(mixed, wo_ref[...], preferred_element_type=F32)
    xn = _rmsnorm(x1, g2_ref[...]).astype(BF16)
    d_ff = wu_ref.shape[1]
    chunk = 1024
    acc = jnp.zeros_like(x1)
    for c in range(d_ff // chunk):
        hid = jnp.dot(xn, wu_ref[:, c * chunk:(c + 1) * chunk], preferred_element_type=F32)
        hid = jnp.square(jnp.maximum(hid, 0.0)).astype(BF16)
        acc = acc + jnp.dot(hid, wd_ref[c * chunk:(c + 1) * chunk, :], preferred_element_type=F32)
    y_ref[...] = _rmsnorm(x1 + acc, gf_ref[...])


def _post(x, attn_inputs, co, w_out, g2, w_up, w_down, gf, combine):
    n, d = x.shape
    tm = min(n, PROMPT_TM)
    assert n % tm == 0
    tok = lambda width: pl.BlockSpec((tm, width), lambda i: (i, 0))
    const = lambda a: pl.BlockSpec(a.shape, lambda i: (0, 0))
    attn_specs = [tok(a.shape[1]) for a in attn_inputs]
    return pl.pallas_call(
        functools.partial(_post_kernel, tm=tm, combine=combine),
        grid=(n // tm,),
        in_specs=[tok(d)] + attn_specs + [tok(CONV_CH), const(w_out), const(g2), const(w_up),
                                          const(w_down), const(gf)],
        out_specs=tok(d),
        out_shape=jax.ShapeDtypeStruct((n, d), F32),
        compiler_params=pltpu.CompilerParams(
            dimension_semantics=("arbitrary",), vmem_limit_bytes=VMEM_LIMIT),
        name="post_prompt" if combine else "post_sample",
    )(x, *attn_inputs, co, w_out, g2, w_up, w_down, gf)


def kernel(x_prompt, x_sample, cache_kv_w128, cache_kv_w512, cache_kv_w2048, state_conv, norm_attn_g, w_in, conv_w, w_out, norm_mlp_g, w_up, w_down, norm_final_g):
    b, t, d = x_prompt.shape
    nb, steps, _ = x_sample.shape
    assert w_in.shape[0] == 1, "single trunk layer"
    g1 = norm_attn_g[0][None, :]
    g2 = norm_mlp_g[0][None, :]
    gf = norm_final_g[None, :]
    w_in_b = w_in[0].astype(BF16)
    w_out_b = w_out[0].astype(BF16)
    w_up_b = w_up[0].astype(BF16)
    w_down_b = w_down[0].astype(BF16)
    cw = conv_w[0]

    q, k, v, co, kv0, kv1, kv2, ctail = _proj_prompt(x_prompt, g1, w_in_b, cw)
    attn = [_attn_prompt(q, k, v, g) for g in range(N_GROUPS)]
    attn_inputs = [o for o, _ in attn] + [l for _, l in attn]
    y_p = _post(x_prompt.reshape(b * t, d), attn_inputs, co.reshape(b * t, CONV_CH),
                w_out_b, g2, w_up_b, w_down_b, gf, True)

    caches = [c[0].reshape(nb, w, KV_W) for c, w in zip((cache_kv_w128, cache_kv_w512, cache_kv_w2048), WINDOWS)]
    state_pad = jnp.pad(state_conv[0], ((0, 0), (0, steps - (CONV_K - 1)), (0, 0))).reshape(nb * steps, CONV_CH)
    q_s, kn, vn, co_s, u_s = _proj_sample(x_sample.reshape(nb * steps, d), g1, w_in_b, cw, state_pad)
    r3 = lambda a: a.reshape(nb, steps, ATTN_W)
    o_s, n0, n1, n2 = _attn_sample(r3(q_s), r3(kn), r3(vn), caches)
    y_s = _post(x_sample.reshape(nb * steps, d), [o_s.reshape(nb * steps, ATTN_W)], co_s,
                w_out_b, g2, w_up_b, w_down_b, gf, False)

    kv_shape = lambda n, w: (1, n, w, 2, HEADS, HEAD_DIM)
    return (y_p.reshape(b, t, d),
            y_s.reshape(nb, steps, d),
            kv0.reshape(kv_shape(b, WINDOWS[0])),
            kv1.reshape(kv_shape(b, WINDOWS[1])),
            kv2.reshape(kv_shape(b, WINDOWS[2])),
            ctail[:, SUBLANES - (CONV_K - 1):, :][None],
            n0.reshape(kv_shape(nb, WINDOWS[0])),
            n1.reshape(kv_shape(nb, WINDOWS[1])),
            n2.reshape(kv_shape(nb, WINDOWS[2])),
            u_s.reshape(nb, steps, CONV_CH)[:, steps - (CONV_K - 1):, :][None])
```

```python
import functools

import jax
import jax.numpy as jnp
from jax import lax
from jax.experimental import pallas as pl
from jax.experimental.pallas import tpu as pltpu

F32 = jnp.float32
BF16 = jnp.bfloat16

EPS = 1e-6
N_GROUPS = 3
HEADS = 4
HEAD_DIM = 64
GROUP_W = HEADS * HEAD_DIM
ATTN_W = N_GROUPS * GROUP_W
QKV_W = 3 * GROUP_W
CONV_CH = 256
CONV_K = 3
KV_W = 2 * GROUP_W
WINDOWS = (128, 512, 2048)
DILATIONS = (1, 4, 16)
SPAN = 128
QK_SCALE = HEAD_DIM ** -0.5

_Q0, _K0, _V0, _B0, _C0, _H0, _END = 0, 768, 1536, 2304, 2560, 2816, 3072

SUBLANES = 8
LANES = 128
VMEM_LIMIT = 56 * 1024 * 1024

PROMPT_TM = 512
ATTN_SB = 2048
TILE = 128
NT_DIMS = (((1,), (1,)), ((), ()))


def _rmsnorm(x, g):
    y = x * lax.rsqrt(jnp.mean(x * x, axis=-1, keepdims=True) + EPS)
    return y * g


def _head_mask(h, dtype):
    lane = lax.broadcasted_iota(jnp.int32, (1, GROUP_W), 1)
    return ((lane >= h * HEAD_DIM) & (lane < (h + 1) * HEAD_DIM)).astype(F32).astype(dtype)


def _in_head(lane, h):
    return (lane >= h * HEAD_DIM) & (lane < (h + 1) * HEAD_DIM)


def _gated_conv(gate_b, u, prev1, prev2, cw_ref):
    y = cw_ref[0:1, :] * prev2
    y = y + cw_ref[1:2, :] * prev1
    y = y + cw_ref[2:3, :] * u
    return gate_b * y


def _proj_prompt_kernel(x_ref, g_ref, w_ref, cw_ref,
                        qkv0_ref, qkv1_ref, qkv2_ref, co_ref, kv0_ref, kv1_ref, kv2_ref, ct_ref,
                        carry_ref, slab_ref, *, tm):
    i = pl.program_id(1)

    @pl.when(i == 0)
    def _():
        carry_ref[...] = jnp.zeros_like(carry_ref)

    xn = _rmsnorm(x_ref[0], g_ref[...]).astype(BF16)

    def proj(lo, hi):
        return jnp.dot(xn, w_ref[:, lo:hi], preferred_element_type=F32)

    qf = proj(_Q0, _K0) * QK_SCALE
    kf = proj(_K0, _V0)
    vf = proj(_V0, _B0)
    for g, (qkv_ref, kv_ref) in enumerate(((qkv0_ref, kv0_ref), (qkv1_ref, kv1_ref), (qkv2_ref, kv2_ref))):
        cols = slice(g * GROUP_W, (g + 1) * GROUP_W)
        rows = kv_ref.shape[1]
        kv_ref[0, :, 0:GROUP_W] = kf[tm - rows:tm, cols]
        kv_ref[0, :, GROUP_W:KV_W] = vf[tm - rows:tm, cols]
        d = DILATIONS[g]
        piece = jnp.concatenate([qf[:, cols], kf[:, cols], vf[:, cols]], axis=1)
        if d == 1:
            qkv_ref[0, 0] = piece.astype(BF16)
        else:
            slabs = slab_ref.at[g - 1]
            for s in range(QKV_W // LANES):
                slabs[s] = piece[:, s * LANES:(s + 1) * LANES]
            for r in range(d):
                for s in range(QKV_W // LANES):
                    qkv_ref[0, r, :, s * LANES:(s + 1) * LANES] = (
                        slabs[s, pl.ds(r, tm // d, stride=d), :].astype(BF16))

    gate_b = proj(_B0, _C0)
    u = proj(_C0, _H0) * proj(_H0, _END)
    row = lax.broadcasted_iota(jnp.int32, (tm, CONV_CH), 0)
    carry = carry_ref[...]
    prev1 = jnp.where(row == 0, carry[7:8, :], pltpu.roll(u, 1, 0))
    prev2 = jnp.where(row == 0, carry[6:7, :],
                      jnp.where(row == 1, carry[7:8, :], pltpu.roll(u, 2, 0)))
    co_ref[0] = _gated_conv(gate_b, u, prev1, prev2, cw_ref).astype(BF16)
    tail = u[tm - SUBLANES:tm, :]
    carry_ref[...] = tail
    ct_ref[0] = tail


def _proj_prompt(x, g, w_in, conv_w):
    b, t, d_model = x.shape
    tm = PROMPT_TM
    nt = t // tm
    assert t % tm == 0 and t >= WINDOWS[-1]

    def tail_spec(w):
        rows = min(w, tm)
        nblk = max(w // tm, 1)
        return (pl.BlockSpec((1, rows, KV_W), lambda bi, i: (bi, jnp.maximum(i - (nt - nblk), 0), 0)),
                jax.ShapeDtypeStruct((b, w, KV_W), F32))

    tails = [tail_spec(w) for w in WINDOWS]
    tok = lambda width: pl.BlockSpec((1, tm, width), lambda bi, i: (bi, i, 0))
    const = lambda shape: pl.BlockSpec(shape, lambda bi, i: (0, 0))
    qkv_specs = [pl.BlockSpec((1, d, tm // d, QKV_W), lambda bi, i: (bi, 0, i, 0)) for d in DILATIONS]
    qkv_shapes = [jax.ShapeDtypeStruct((b, d, t // d, QKV_W), BF16) for d in DILATIONS]
    return pl.pallas_call(
        functools.partial(_proj_prompt_kernel, tm=tm),
        grid=(b, nt),
        in_specs=[tok(d_model), const((1, d_model)), const((d_model, _END)), const((CONV_K, CONV_CH))],
        out_specs=qkv_specs + [tok(CONV_CH)] + [s for s, _ in tails]
                  + [pl.BlockSpec((1, SUBLANES, CONV_CH), lambda bi, i: (bi, 0, 0))],
        out_shape=qkv_shapes + [jax.ShapeDtypeStruct((b, t, CONV_CH), BF16)] + [s for _, s in tails]
                  + [jax.ShapeDtypeStruct((b, SUBLANES, CONV_CH), F32)],
        scratch_shapes=[pltpu.VMEM((SUBLANES, CONV_CH), F32),
                        pltpu.VMEM((N_GROUPS - 1, QKV_W // LANES, tm, LANES), F32)],
        compiler_params=pltpu.CompilerParams(
            dimension_semantics=("arbitrary", "arbitrary"), vmem_limit_bytes=VMEM_LIMIT),
        name="proj_prompt",
    )(x, g, w_in, conv_w)


def _proj_sample_kernel(x_ref, g_ref, w_ref, cw_ref, sp_ref,
                        q_ref, kn_ref, vn_ref, co_ref, u_ref, *, tm):
    xn = _rmsnorm(x_ref[...], g_ref[...]).astype(BF16)

    def proj(lo, hi):
        return jnp.dot(xn, w_ref[:, lo:hi], preferred_element_type=F32)

    q_ref[...] = proj(_Q0, _K0) * QK_SCALE
    kn_ref[...] = proj(_K0, _V0)
    vn_ref[...] = proj(_V0, _B0)
    gate_b = proj(_B0, _C0)
    u = proj(_C0, _H0) * proj(_H0, _END)
    step = lax.broadcasted_iota(jnp.int32, (tm, CONV_CH), 0) & (SUBLANES - 1)
    sp = sp_ref[...]
    prev1 = jnp.where(step == 0, pltpu.roll(sp, tm - 1, 0), pltpu.roll(u, 1, 0))
    prev2 = jnp.where(step < 2, sp, pltpu.roll(u, 2, 0))
    co_ref[...] = _gated_conv(gate_b, u, prev1, prev2, cw_ref).astype(BF16)
    u_ref[...] = u


def _proj_sample(x, g, w_in, conv_w, state_pad):
    n, d_model = x.shape
    tm = min(n, PROMPT_TM)
    assert n % tm == 0 and tm % SUBLANES == 0
    tok = lambda width: pl.BlockSpec((tm, width), lambda i: (i, 0))
    const = lambda shape: pl.BlockSpec(shape, lambda i: (0, 0))
    return pl.pallas_call(
        functools.partial(_proj_sample_kernel, tm=tm),
        grid=(n // tm,),
        in_specs=[tok(d_model), const((1, d_model)), const((d_model, _END)), const((CONV_K, CONV_CH)),
                  tok(CONV_CH)],
        out_specs=[tok(ATTN_W), tok(ATTN_W), tok(ATTN_W), tok(CONV_CH), tok(CONV_CH)],
        out_shape=[jax.ShapeDtypeStruct((n, ATTN_W), F32)] * 3
                  + [jax.ShapeDtypeStruct((n, CONV_CH), BF16), jax.ShapeDtypeStruct((n, CONV_CH), F32)],
        compiler_params=pltpu.CompilerParams(
            dimension_semantics=("arbitrary",), vmem_limit_bytes=VMEM_LIMIT),
        name="proj_sample",
    )(x, g, w_in, conv_w, state_pad)


def _attn_prompt_kernel(qkv_ref, kp_ref, vp_ref, o_ref, l_ref, *, d, n_tiles):
    sb = pl.program_id(1)

    row = lax.broadcasted_iota(jnp.int32, (TILE, TILE + SPAN), 0)
    col = lax.broadcasted_iota(jnp.int32, (TILE, TILE + SPAN), 1)
    band = (col >= row) & (col <= row + SPAN)
    band_first = band & (col + jnp.where(sb == 0, 0, SPAN) >= SPAN)
    lane128 = lax.broadcasted_iota(jnp.int32, (TILE, LANES), 1)
    lane256 = lax.broadcasted_iota(jnp.int32, (TILE, GROUP_W), 1)
    k_cols = slice(GROUP_W, 2 * GROUP_W)
    v_cols = slice(2 * GROUP_W, 3 * GROUP_W)

    def keys_values(r, j, cols, prev_ref):
        if j == 0:
            return jnp.concatenate([prev_ref[0, r], qkv_ref[0, r, 0:TILE, cols]], axis=0)
        return qkv_ref[0, r, (j - 1) * TILE:(j + 1) * TILE, cols]

    def scores(r, j):
        q = qkv_ref[0, r, j * TILE:(j + 1) * TILE, 0:GROUP_W]
        kk = keys_values(r, j, k_cols, kp_ref)
        return [lax.dot_general(q * _head_mask(h, BF16), kk, NT_DIMS, preferred_element_type=F32)
                for h in range(HEADS)]

    def finish(r, j, ss):
        vv = keys_values(r, j, v_cols, vp_ref)
        valid = band_first if j == 0 else band
        o_acc = jnp.zeros((TILE, GROUP_W), F32)
        l_acc = jnp.zeros((TILE, LANES), F32)
        for h in range(HEADS):
            s = jnp.where(valid, ss[h], -jnp.inf)
            m = jnp.max(s, axis=-1, keepdims=True)
            p = jnp.exp(s - m)
            z = jnp.sum(p, axis=-1, keepdims=True)
            of = jnp.dot(p.astype(BF16), vv, preferred_element_type=F32)
            o_acc = jnp.where(_in_head(lane256, h), of / z, o_acc)
            l_acc = jnp.where(lane128 == h, m + jnp.log(z), l_acc)
        if d == 1:
            rows = pl.ds(j * TILE, TILE)
        else:
            rows = pl.ds(j * TILE * d + r, TILE, stride=d)
        o_ref[0, 0, rows, :] = o_acc[:, 0:LANES]
        o_ref[0, 1, rows, :] = o_acc[:, LANES:GROUP_W]
        l_ref[0, rows, :] = l_acc

    units = [(r, j) for r in range(d) for j in range(n_tiles)]
    ss_next = scores(*units[0])
    for n, unit in enumerate(units):
        ss = ss_next
        if n + 1 < len(units):
            ss_next = scores(*units[n + 1])
        finish(*unit, ss)


def _attn_prompt(qkv, g):
    b, d, tc, _ = qkv.shape
    t = tc * d
    sb = min(ATTN_SB, t)
    n_tiles = sb // d // TILE
    assert d == DILATIONS[g] and t % sb == 0 and sb % (d * TILE) == 0 and TILE == SPAN

    def prev_spec(col_block):
        return pl.BlockSpec((1, d, SPAN, GROUP_W),
                            lambda bi, s: (bi, 0, jnp.maximum(s * n_tiles - 1, 0), col_block))

    return pl.pallas_call(
        functools.partial(_attn_prompt_kernel, d=d, n_tiles=n_tiles),
        grid=(b, t // sb),
        in_specs=[pl.BlockSpec((1, d, sb // d, QKV_W), lambda bi, s: (bi, 0, s, 0)),
                  prev_spec(1), prev_spec(2)],
        out_specs=[pl.BlockSpec((1, 2, sb, LANES), lambda bi, s: (bi, 0, s, 0)),
                   pl.BlockSpec((1, sb, LANES), lambda bi, s: (bi, s, 0))],
        out_shape=[jax.ShapeDtypeStruct((b, 2, t, LANES), F32),
                   jax.ShapeDtypeStruct((b, t, LANES), F32)],
        compiler_params=pltpu.CompilerParams(
            dimension_semantics=("arbitrary", "arbitrary"), vmem_limit_bytes=VMEM_LIMIT),
        name=f"attn_prompt_g{g}",
    )(qkv, qkv, qkv)


def _attn_sample_kernel(q_ref, kn_ref, vn_ref, c0_ref, c1_ref, c2_ref,
                        o_ref, n0_ref, n1_ref, n2_ref, *, steps):
    rows = HEADS * steps
    step_of = lambda shape: lax.broadcasted_iota(jnp.int32, shape, 0) & (steps - 1)
    lane256 = lax.broadcasted_iota(jnp.int32, (steps, GROUP_W), 1)
    lane_kv = lax.broadcasted_iota(jnp.int32, (KV_W, LANES), 1)
    pad = jnp.zeros((LANES - steps, GROUP_W), F32)
    outs, lses = [], []
    for g, (c_ref, n_ref) in enumerate(((c0_ref, n0_ref), (c1_ref, n1_ref), (c2_ref, n2_ref))):
        w = c_ref.shape[2]
        d = DILATIONS[g]
        kn = kn_ref[0, :, g * GROUP_W:(g + 1) * GROUP_W]
        vn = vn_ref[0, :, g * GROUP_W:(g + 1) * GROUP_W]
        knt = jnp.concatenate([kn, pad], axis=0).T
        vnt = jnp.concatenate([vn, pad], axis=0).T
        shift = LANES - steps
        nxt = pltpu.roll(c_ref[0, :, 0:LANES], shift, 1)
        for cb in range(w // LANES):
            cur = nxt
            if cb + 1 < w // LANES:
                nxt = pltpu.roll(c_ref[0, :, (cb + 1) * LANES:(cb + 2) * LANES], shift, 1)
            else:
                nxt = pltpu.roll(jnp.concatenate([knt, vnt], axis=0), shift, 1)
            n_ref[0, :, cb * LANES:(cb + 1) * LANES] = jnp.where(lane_kv < shift, cur, nxt)

        qg = q_ref[0, :, g * GROUP_W:(g + 1) * GROUP_W]
        qs = jnp.concatenate([qg * _head_mask(h, F32) for h in range(HEADS)], axis=0).astype(BF16)
        ck = c_ref[0, 0:GROUP_W, :].astype(BF16)
        cv = c_ref[0, GROUP_W:KV_W, :].astype(BF16)
        s_c = jnp.dot(qs, ck, preferred_element_type=F32)
        s_n = jnp.dot(qs, knt.astype(BF16), preferred_element_type=F32)
        t_c = step_of((rows, w))
        i_c = lax.broadcasted_iota(jnp.int32, (rows, w), 1)
        valid_c = (i_c >= t_c) & (((i_c - t_c) & (d - 1)) == 0)
        t_n = step_of((rows, LANES))
        i_n = lax.broadcasted_iota(jnp.int32, (rows, LANES), 1)
        valid_n = (i_n <= t_n) & (((t_n - i_n) & (d - 1)) == 0)
        s_c = jnp.where(valid_c, s_c, -jnp.inf)
        s_n = jnp.where(valid_n, s_n, -jnp.inf)
        m = jnp.maximum(jnp.max(s_c, axis=-1, keepdims=True), jnp.max(s_n, axis=-1, keepdims=True))
        p_c = jnp.exp(s_c - m)
        p_n = jnp.exp(s_n - m)
        z = jnp.sum(p_c, axis=-1, keepdims=True) + jnp.sum(p_n, axis=-1, keepdims=True)
        of = lax.dot_general(p_c.astype(BF16), cv, NT_DIMS, preferred_element_type=F32)
        of = of + lax.dot_general(p_n.astype(BF16), vnt.astype(BF16), NT_DIMS, preferred_element_type=F32)
        outs.append(of / z)
        lses.append(m + jnp.log(z))

    top = jnp.maximum(jnp.maximum(lses[0], lses[1]), lses[2])
    es = [jnp.exp(l - top) for l in lses]
    den = es[0] + es[1] + es[2]
    for g in range(N_GROUPS):
        og = outs[g] * (es[g] / den)
        res = jnp.zeros((steps, GROUP_W), F32)
        for h in range(HEADS):
            res = jnp.where(_in_head(lane256, h), og[h * steps:(h + 1) * steps, :], res)
        o_ref[0, :, g * GROUP_W:(g + 1) * GROUP_W] = res


def _attn_sample(q, kn, vn, caches):
    nb, steps, _ = q.shape
    assert steps == SUBLANES and all(c.shape[1:] == (KV_W, w) for c, w in zip(caches, WINDOWS))
    tok = pl.BlockSpec((1, steps, ATTN_W), lambda i: (i, 0, 0))
    cspecs = [pl.BlockSpec((1, KV_W, w), lambda i: (i, 0, 0)) for w in WINDOWS]
    return pl.pallas_call(
        functools.partial(_attn_sample_kernel, steps=steps),
        grid=(nb,),
        in_specs=[tok, tok, tok] + cspecs,
        out_specs=[tok] + cspecs,
        out_shape=[jax.ShapeDtypeStruct((nb, steps, ATTN_W), F32)]
                  + [jax.ShapeDtypeStruct((nb, KV_W, w), F32) for w in WINDOWS],
        compiler_params=pltpu.CompilerParams(
            dimension_semantics=("arbitrary",), vmem_limit_bytes=VMEM_LIMIT),
        name="attn_sample",
    )(q, kn, vn, *caches)


def _post_kernel(*refs, tm, combine):
    if combine:
        (x_ref, o0_ref, o1_ref, o2_ref, l0_ref, l1_ref, l2_ref, co_ref,
         wo_ref, g2_ref, wu_ref, wd_ref, gf_ref, y_ref) = refs
        ls = [l0_ref[0], l1_ref[0], l2_ref[0]]
        top = jnp.maximum(jnp.maximum(ls[0], ls[1]), ls[2])
        es = [jnp.exp(l - top) for l in ls]
        den = es[0] + es[1] + es[2]
        lane = lax.broadcasted_iota(jnp.int32, (tm, GROUP_W), 1)
        parts = []
        for o_ref, e in zip((o0_ref, o1_ref, o2_ref), es):
            alpha = e / den
            wide = jnp.zeros((tm, GROUP_W), F32)
            for h in range(HEADS):
                wide = jnp.where(_in_head(lane, h), alpha[:, h:h + 1], wide)
            o = jnp.concatenate([o_ref[0, 0], o_ref[0, 1]], axis=1)
            parts.append((o * wide).astype(BF16))
        mixed = jnp.concatenate(parts + [co_ref[0]], axis=-1)
    else:
        x_ref, o_ref, co_ref, wo_ref, g2_ref, wu_ref, wd_ref, gf_ref, y_ref = refs
        mixed = jnp.concatenate([o_ref[0].astype(BF16), co_ref[0]], axis=-1)

    x1 = x_ref[0] + jnp.dot(mixed, wo_ref[...], preferred_element_type=F32)
    xn = _rmsnorm(x1, g2_ref[...]).astype(BF16)
    chunk = 1024
    n_chunks = wu_ref.shape[1] // chunk

    def up(c):
        hid = jnp.dot(xn, wu_ref[:, c * chunk:(c + 1) * chunk], preferred_element_type=F32)
        return jnp.square(jnp.maximum(hid, 0.0)).astype(BF16)

    acc = jnp.zeros_like(x1)
    hid_next = up(0)
    for c in range(n_chunks):
        hid = hid_next
        if c + 1 < n_chunks:
            hid_next = up(c + 1)
        acc = acc + jnp.dot(hid, wd_ref[c * chunk:(c + 1) * chunk, :], preferred_element_type=F32)
    y_ref[0] = _rmsnorm(x1 + acc, gf_ref[...])


def _post(x, attn_inputs, co, w_out, g2, w_up, w_down, gf, combine):
    b, t, d_model = x.shape
    tm = min(t, PROMPT_TM)
    assert t % tm == 0

    def tok(a):
        if a.ndim == 4:
            return pl.BlockSpec((1, a.shape[1], tm, a.shape[3]), lambda bi, i: (bi, 0, i, 0))
        return pl.BlockSpec((1, tm, a.shape[2]), lambda bi, i: (bi, i, 0))

    const = lambda a: pl.BlockSpec(a.shape, lambda bi, i: (0, 0))
    return pl.pallas_call(
        functools.partial(_post_kernel, tm=tm, combine=combine),
        grid=(b, t // tm),
        in_specs=[tok(x)] + [tok(a) for a in attn_inputs]
                 + [tok(co), const(w_out), const(g2), const(w_up), const(w_down), const(gf)],
        out_specs=tok(x),
        out_shape=jax.ShapeDtypeStruct((b, t, d_model), F32),
        compiler_params=pltpu.CompilerParams(
            dimension_semantics=("arbitrary", "arbitrary"), vmem_limit_bytes=VMEM_LIMIT),
        name="post_prompt" if combine else "post_sample",
    )(x, *attn_inputs, co, w_out, g2, w_up, w_down, gf)


def kernel(x_prompt, x_sample, cache_kv_w128, cache_kv_w512, cache_kv_w2048, state_conv, norm_attn_g, w_in, conv_w, w_out, norm_mlp_g, w_up, w_down, norm_final_g):
    b, t, d_model = x_prompt.shape
    nb, steps, _ = x_sample.shape
    n_s = nb * steps
    assert w_in.shape[0] == 1, "single trunk layer"
    g1 = norm_attn_g[0][None, :]
    g2 = norm_mlp_g[0][None, :]
    gf = norm_final_g[None, :]
    w_in_b = w_in[0].astype(BF16)
    w_out_b = w_out[0].astype(BF16)
    w_up_b = w_up[0].astype(BF16)
    w_down_b = w_down[0].astype(BF16)
    cw = conv_w[0]

    caches = [jnp.transpose(c[0], (0, 2, 3, 4, 1)).reshape(nb, KV_W, w)
              for c, w in zip((cache_kv_w128, cache_kv_w512, cache_kv_w2048), WINDOWS)]
    state_pad = jnp.pad(state_conv[0], ((0, 0), (0, steps - (CONV_K - 1)), (0, 0))).reshape(n_s, CONV_CH)
    q_s, kn, vn, co_s, u_s = _proj_sample(x_sample.reshape(n_s, d_model), g1, w_in_b, cw, state_pad)
    r3 = lambda a: a.reshape(nb, steps, ATTN_W)
    o_s, n0, n1, n2 = _attn_sample(r3(q_s), r3(kn), r3(vn), caches)
    y_s = _post(x_sample.reshape(1, n_s, d_model), [o_s.reshape(1, n_s, ATTN_W)], co_s.reshape(1, n_s, CONV_CH),
                w_out_b, g2, w_up_b, w_down_b, gf, False)

    qkv0, qkv1, qkv2, co, kv0, kv1, kv2, ctail = _proj_prompt(x_prompt, g1, w_in_b, cw)
    attn = [_attn_prompt(qkv, g) for g, qkv in enumerate((qkv0, qkv1, qkv2))]
    attn_inputs = [o for o, _ in attn] + [l for _, l in attn]
    y_p = _post(x_prompt, attn_inputs, co, w_out_b, g2, w_up_b, w_down_b, gf, True)

    kv_p = lambda a, w: a.reshape(1, b, w, 2, HEADS, HEAD_DIM)
    kv_s = lambda a, w: jnp.transpose(a.reshape(nb, 2, HEADS, HEAD_DIM, w), (0, 4, 1, 2, 3))[None]
    return (y_p,
            y_s.reshape(nb, steps, d_model),
            kv_p(kv0, WINDOWS[0]),
            kv_p(kv1, WINDOWS[1]),
            kv_p(kv2, WINDOWS[2]),
            ctail[:, SUBLANES - (CONV_K - 1):, :][None],
            kv_s(n0, WINDOWS[0]),
            kv_s(n1, WINDOWS[1]),
            kv_s(n2, WINDOWS[2]),
            u_s.reshape(nb, steps, CONV_CH)[:, steps - (CONV_K - 1):, :][None])
```

```python
import functools

import jax
import jax.numpy as jnp
from jax import lax
from jax.experimental import pallas as pl
from jax.experimental.pallas import tpu as pltpu

F32 = jnp.float32
BF16 = jnp.bfloat16

EPS = 1e-6
N_GROUPS = 3
HEADS = 4
HEAD_DIM = 64
GROUP_W = HEADS * HEAD_DIM
ATTN_W = N_GROUPS * GROUP_W
QKV_W = 3 * GROUP_W
CONV_CH = 256
CONV_K = 3
KV_W = 2 * GROUP_W
WINDOWS = (128, 512, 2048)
DILATIONS = (1, 4, 16)
SPAN = 128
QK_SCALE = HEAD_DIM ** -0.5

_Q0, _K0, _V0, _B0, _C0, _H0, _END = 0, 768, 1536, 2304, 2560, 2816, 3072

SUBLANES = 8
LANES = 128
VMEM_LIMIT = 56 * 1024 * 1024

PROJ_TM = 1024
POST_TM = 512
ATTN_SB = 2048
TILE = 128
NT_DIMS = (((1,), (1,)), ((), ()))


def _rmsnorm(x, g):
    y = x * lax.rsqrt(jnp.mean(x * x, axis=-1, keepdims=True) + EPS)
    return y * g


def _head_mask(h, dtype):
    lane = lax.broadcasted_iota(jnp.int32, (1, GROUP_W), 1)
    return ((lane >= h * HEAD_DIM) & (lane < (h + 1) * HEAD_DIM)).astype(F32).astype(dtype)


def _in_head(lane, h):
    return (lane >= h * HEAD_DIM) & (lane < (h + 1) * HEAD_DIM)


def _gated_conv(gate_b, u, prev1, prev2, cw_ref):
    y = cw_ref[0:1, :] * prev2
    y = y + cw_ref[1:2, :] * prev1
    y = y + cw_ref[2:3, :] * u
    return gate_b * y


def _proj_prompt_kernel(x_ref, g_ref, w_ref, cw_ref,
                        qkv0_ref, qkv1_ref, qkv2_ref, co_ref, kv0_ref, kv1_ref, kv2_ref, ct_ref,
                        carry_ref, slab_ref, *, tm):
    i = pl.program_id(1)

    @pl.when(i == 0)
    def _():
        carry_ref[...] = jnp.zeros_like(carry_ref)

    xn = _rmsnorm(x_ref[0], g_ref[...]).astype(BF16)

    def proj(lo, hi):
        return jnp.dot(xn, w_ref[:, lo:hi], preferred_element_type=F32)

    qf = proj(_Q0, _K0) * QK_SCALE
    kf = proj(_K0, _V0)
    vf = proj(_V0, _B0)
    for g, (qkv_ref, kv_ref) in enumerate(((qkv0_ref, kv0_ref), (qkv1_ref, kv1_ref), (qkv2_ref, kv2_ref))):
        cols = slice(g * GROUP_W, (g + 1) * GROUP_W)
        rows = kv_ref.shape[1]
        kv_ref[0, :, 0:GROUP_W] = kf[tm - rows:tm, cols]
        kv_ref[0, :, GROUP_W:KV_W] = vf[tm - rows:tm, cols]
        d = DILATIONS[g]
        piece = jnp.concatenate([qf[:, cols], kf[:, cols], vf[:, cols]], axis=1)
        if d == 1:
            qkv_ref[0, 0] = piece.astype(BF16)
        else:
            slabs = slab_ref.at[g - 1]
            for s in range(QKV_W // LANES):
                slabs[s] = piece[:, s * LANES:(s + 1) * LANES]
            for r in range(d):
                for s in range(QKV_W // LANES):
                    qkv_ref[0, r, :, s * LANES:(s + 1) * LANES] = (
                        slabs[s, pl.ds(r, tm // d, stride=d), :].astype(BF16))

    gate_b = proj(_B0, _C0)
    u = proj(_C0, _H0) * proj(_H0, _END)
    row = lax.broadcasted_iota(jnp.int32, (tm, CONV_CH), 0)
    carry = carry_ref[...]
    prev1 = jnp.where(row == 0, carry[7:8, :], pltpu.roll(u, 1, 0))
    prev2 = jnp.where(row == 0, carry[6:7, :],
                      jnp.where(row == 1, carry[7:8, :], pltpu.roll(u, 2, 0)))
    co_ref[0] = _gated_conv(gate_b, u, prev1, prev2, cw_ref).astype(BF16)
    tail = u[tm - SUBLANES:tm, :]
    carry_ref[...] = tail
    ct_ref[0] = tail


def _proj_prompt(x, g, w_in, conv_w):
    b, t, d_model = x.shape
    tm = PROJ_TM
    nt = t // tm
    assert t % tm == 0 and t >= WINDOWS[-1]

    def tail_spec(w):
        rows = min(w, tm)
        nblk = max(w // tm, 1)
        return (pl.BlockSpec((1, rows, KV_W), lambda bi, i: (bi, jnp.maximum(i - (nt - nblk), 0), 0)),
                jax.ShapeDtypeStruct((b, w, KV_W), F32))

    tails = [tail_spec(w) for w in WINDOWS]
    tok = lambda width: pl.BlockSpec((1, tm, width), lambda bi, i: (bi, i, 0))
    const = lambda shape: pl.BlockSpec(shape, lambda bi, i: (0, 0))
    qkv_specs = [pl.BlockSpec((1, d, tm // d, QKV_W), lambda bi, i: (bi, 0, i, 0)) for d in DILATIONS]
    qkv_shapes = [jax.ShapeDtypeStruct((b, d, t // d, QKV_W), BF16) for d in DILATIONS]
    return pl.pallas_call(
        functools.partial(_proj_prompt_kernel, tm=tm),
        grid=(b, nt),
        in_specs=[tok(d_model), const((1, d_model)), const((d_model, _END)), const((CONV_K, CONV_CH))],
        out_specs=qkv_specs + [tok(CONV_CH)] + [s for s, _ in tails]
                  + [pl.BlockSpec((1, SUBLANES, CONV_CH), lambda bi, i: (bi, 0, 0))],
        out_shape=qkv_shapes + [jax.ShapeDtypeStruct((b, t, CONV_CH), BF16)] + [s for _, s in tails]
                  + [jax.ShapeDtypeStruct((b, SUBLANES, CONV_CH), F32)],
        scratch_shapes=[pltpu.VMEM((SUBLANES, CONV_CH), F32),
                        pltpu.VMEM((N_GROUPS - 1, QKV_W // LANES, tm, LANES), F32)],
        compiler_params=pltpu.CompilerParams(
            dimension_semantics=("arbitrary", "arbitrary"), vmem_limit_bytes=VMEM_LIMIT),
        name="proj_prompt",
    )(x, g, w_in, conv_w)


def _proj_sample_kernel(x_ref, g_ref, w_ref, cw_ref, sp_ref,
                        q_ref, kn_ref, vn_ref, co_ref, u_ref, *, tm):
    xn = _rmsnorm(x_ref[...], g_ref[...]).astype(BF16)

    def proj(lo, hi):
        return jnp.dot(xn, w_ref[:, lo:hi], preferred_element_type=F32)

    q_ref[...] = proj(_Q0, _K0) * QK_SCALE
    kn_ref[...] = proj(_K0, _V0)
    vn_ref[...] = proj(_V0, _B0)
    gate_b = proj(_B0, _C0)
    u = proj(_C0, _H0) * proj(_H0, _END)
    step = lax.broadcasted_iota(jnp.int32, (tm, CONV_CH), 0) & (SUBLANES - 1)
    sp = sp_ref[...]
    prev1 = jnp.where(step == 0, pltpu.roll(sp, tm - 1, 0), pltpu.roll(u, 1, 0))
    prev2 = jnp.where(step < 2, sp, pltpu.roll(u, 2, 0))
    co_ref[...] = _gated_conv(gate_b, u, prev1, prev2, cw_ref).astype(BF16)
    u_ref[...] = u


def _proj_sample(x, g, w_in, conv_w, state_pad):
    n, d_model = x.shape
    tm = min(n, POST_TM)
    assert n % tm == 0 and tm % SUBLANES == 0
    tok = lambda width: pl.BlockSpec((tm, width), lambda i: (i, 0))
    const = lambda shape: pl.BlockSpec(shape, lambda i: (0, 0))
    return pl.pallas_call(
        functools.partial(_proj_sample_kernel, tm=tm),
        grid=(n // tm,),
        in_specs=[tok(d_model), const((1, d_model)), const((d_model, _END)), const((CONV_K, CONV_CH)),
                  tok(CONV_CH)],
        out_specs=[tok(ATTN_W), tok(ATTN_W), tok(ATTN_W), tok(CONV_CH), tok(CONV_CH)],
        out_shape=[jax.ShapeDtypeStruct((n, ATTN_W), F32)] * 3
                  + [jax.ShapeDtypeStruct((n, CONV_CH), BF16), jax.ShapeDtypeStruct((n, CONV_CH), F32)],
        compiler_params=pltpu.CompilerParams(
            dimension_semantics=("arbitrary",), vmem_limit_bytes=VMEM_LIMIT),
        name="proj_sample",
    )(x, g, w_in, conv_w, state_pad)


def _attn_prompt_kernel(qkv_ref, kp_ref, vp_ref, o_ref, l_ref, *, d, n_tiles):
    sb = pl.program_id(1)

    row = lax.broadcasted_iota(jnp.int32, (TILE, TILE + SPAN), 0)
    col = lax.broadcasted_iota(jnp.int32, (TILE, TILE + SPAN), 1)
    band = (col >= row) & (col <= row + SPAN)
    band_first = band & (col + jnp.where(sb == 0, 0, SPAN) >= SPAN)
    lane256 = lax.broadcasted_iota(jnp.int32, (TILE, GROUP_W), 1)
    k_cols = slice(GROUP_W, 2 * GROUP_W)
    v_cols = slice(2 * GROUP_W, 3 * GROUP_W)

    def keys_values(r, j, cols, prev_ref):
        if j == 0:
            return jnp.concatenate([prev_ref[0, r], qkv_ref[0, r, 0:TILE, cols]], axis=0)
        return qkv_ref[0, r, (j - 1) * TILE:(j + 1) * TILE, cols]

    def scores(r, j):
        q = qkv_ref[0, r, j * TILE:(j + 1) * TILE, 0:GROUP_W]
        kk = keys_values(r, j, k_cols, kp_ref)
        return [lax.dot_general(q * _head_mask(h, BF16), kk, NT_DIMS, preferred_element_type=F32)
                for h in range(HEADS)]

    def finish(r, j, ss):
        vv = keys_values(r, j, v_cols, vp_ref)
        valid = band_first if j == 0 else band
        o_acc = jnp.zeros((TILE, GROUP_W), F32)
        l_acc = jnp.zeros((TILE, GROUP_W), F32)
        for h in range(HEADS):
            s = jnp.where(valid, ss[h], -jnp.inf)
            m = jnp.max(s, axis=-1, keepdims=True)
            p = jnp.exp(s - m)
            z = jnp.sum(p, axis=-1, keepdims=True)
            of = jnp.dot(p.astype(BF16), vv, preferred_element_type=F32)
            o_acc = jnp.where(_in_head(lane256, h), of / z, o_acc)
            l_acc = jnp.where(_in_head(lane256, h), m + jnp.log(z), l_acc)
        if d == 1:
            rows = pl.ds(j * TILE, TILE)
        else:
            rows = pl.ds(j * TILE * d + r, TILE, stride=d)
        for half in range(GROUP_W // LANES):
            lanes = slice(half * LANES, (half + 1) * LANES)
            o_ref[0, half, rows, :] = o_acc[:, lanes]
            l_ref[0, half, rows, :] = l_acc[:, lanes]

    units = [(r, j) for r in range(d) for j in range(n_tiles)]
    ss_next = scores(*units[0])
    for n, unit in enumerate(units):
        ss = ss_next
        if n + 1 < len(units):
            ss_next = scores(*units[n + 1])
        finish(*unit, ss)


def _attn_prompt(qkv, g):
    b, d, tc, _ = qkv.shape
    t = tc * d
    sb = min(ATTN_SB, t)
    n_tiles = sb // d // TILE
    assert d == DILATIONS[g] and t % sb == 0 and sb % (d * TILE) == 0 and TILE == SPAN

    def prev_spec(col_block):
        return pl.BlockSpec((1, d, SPAN, GROUP_W),
                            lambda bi, s: (bi, 0, jnp.maximum(s * n_tiles - 1, 0), col_block))

    return pl.pallas_call(
        functools.partial(_attn_prompt_kernel, d=d, n_tiles=n_tiles),
        grid=(b, t // sb),
        in_specs=[pl.BlockSpec((1, d, sb // d, QKV_W), lambda bi, s: (bi, 0, s, 0)),
                  prev_spec(1), prev_spec(2)],
        out_specs=[pl.BlockSpec((1, 2, sb, LANES), lambda bi, s: (bi, 0, s, 0))] * 2,
        out_shape=[jax.ShapeDtypeStruct((b, 2, t, LANES), F32)] * 2,
        compiler_params=pltpu.CompilerParams(
            dimension_semantics=("arbitrary", "arbitrary"), vmem_limit_bytes=VMEM_LIMIT),
        name=f"attn_prompt_g{g}",
    )(qkv, qkv, qkv)


def _attn_sample_kernel(q_ref, kn_ref, vn_ref, c0_ref, c1_ref, c2_ref,
                        o_ref, n0_ref, n1_ref, n2_ref, *, steps):
    rows = HEADS * steps
    step_of = lambda shape: lax.broadcasted_iota(jnp.int32, shape, 0) & (steps - 1)
    lane256 = lax.broadcasted_iota(jnp.int32, (steps, GROUP_W), 1)
    lane_kv = lax.broadcasted_iota(jnp.int32, (KV_W, LANES), 1)
    pad = jnp.zeros((LANES - steps, GROUP_W), F32)
    outs, lses = [], []
    for g, (c_ref, n_ref) in enumerate(((c0_ref, n0_ref), (c1_ref, n1_ref), (c2_ref, n2_ref))):
        w = c_ref.shape[2]
        d = DILATIONS[g]
        kn = kn_ref[0, :, g * GROUP_W:(g + 1) * GROUP_W]
        vn = vn_ref[0, :, g * GROUP_W:(g + 1) * GROUP_W]
        knt = jnp.concatenate([kn, pad], axis=0).T
        vnt = jnp.concatenate([vn, pad], axis=0).T
        shift = LANES - steps
        nxt = pltpu.roll(c_ref[0, :, 0:LANES], shift, 1)
        for cb in range(w // LANES):
            cur = nxt
            if cb + 1 < w // LANES:
                nxt = pltpu.roll(c_ref[0, :, (cb + 1) * LANES:(cb + 2) * LANES], shift, 1)
            else:
                nxt = pltpu.roll(jnp.concatenate([knt, vnt], axis=0), shift, 1)
            n_ref[0, :, cb * LANES:(cb + 1) * LANES] = jnp.where(lane_kv < shift, cur, nxt)

        qg = q_ref[0, :, g * GROUP_W:(g + 1) * GROUP_W]
        qs = jnp.concatenate([qg * _head_mask(h, F32) for h in range(HEADS)], axis=0).astype(BF16)
        ck = c_ref[0, 0:GROUP_W, :].astype(BF16)
        cv = c_ref[0, GROUP_W:KV_W, :].astype(BF16)
        s_c = jnp.dot(qs, ck, preferred_element_type=F32)
        s_n = jnp.dot(qs, knt.astype(BF16), preferred_element_type=F32)
        t_c = step_of((rows, w))
        i_c = lax.broadcasted_iota(jnp.int32, (rows, w), 1)
        valid_c = (i_c >= t_c) & (((i_c - t_c) & (d - 1)) == 0)
        t_n = step_of((rows, LANES))
        i_n = lax.broadcasted_iota(jnp.int32, (rows, LANES), 1)
        valid_n = (i_n <= t_n) & (((t_n - i_n) & (d - 1)) == 0)
        s_c = jnp.where(valid_c, s_c, -jnp.inf)
        s_n = jnp.where(valid_n, s_n, -jnp.inf)
        m = jnp.maximum(jnp.max(s_c, axis=-1, keepdims=True), jnp.max(s_n, axis=-1, keepdims=True))
        p_c = jnp.exp(s_c - m)
        p_n = jnp.exp(s_n - m)
        z = jnp.sum(p_c, axis=-1, keepdims=True) + jnp.sum(p_n, axis=-1, keepdims=True)
        of = lax.dot_general(p_c.astype(BF16), cv, NT_DIMS, preferred_element_type=F32)
        of = of + lax.dot_general(p_n.astype(BF16), vnt.astype(BF16), NT_DIMS, preferred_element_type=F32)
        outs.append(of / z)
        lses.append(m + jnp.log(z))

    top = jnp.maximum(jnp.maximum(lses[0], lses[1]), lses[2])
    es = [jnp.exp(l - top) for l in lses]
    den = es[0] + es[1] + es[2]
    for g in range(N_GROUPS):
        og = outs[g] * (es[g] / den)
        res = jnp.zeros((steps, GROUP_W), F32)
        for h in range(HEADS):
            res = jnp.where(_in_head(lane256, h), og[h * steps:(h + 1) * steps, :], res)
        o_ref[0, :, g * GROUP_W:(g + 1) * GROUP_W] = res


def _attn_sample(q, kn, vn, caches):
    nb, steps, _ = q.shape
    assert steps == SUBLANES and all(c.shape[1:] == (KV_W, w) for c, w in zip(caches, WINDOWS))
    tok = pl.BlockSpec((1, steps, ATTN_W), lambda i: (i, 0, 0))
    cspecs = [pl.BlockSpec((1, KV_W, w), lambda i: (i, 0, 0)) for w in WINDOWS]
    return pl.pallas_call(
        functools.partial(_attn_sample_kernel, steps=steps),
        grid=(nb,),
        in_specs=[tok, tok, tok] + cspecs,
        out_specs=[tok] + cspecs,
        out_shape=[jax.ShapeDtypeStruct((nb, steps, ATTN_W), F32)]
                  + [jax.ShapeDtypeStruct((nb, KV_W, w), F32) for w in WINDOWS],
        compiler_params=pltpu.CompilerParams(
            dimension_semantics=("arbitrary",), vmem_limit_bytes=VMEM_LIMIT),
        name="attn_sample",
    )(q, kn, vn, *caches)


def _post_kernel(*refs, tm, combine):
    if combine:
        (x_ref, o0_ref, o1_ref, o2_ref, l0_ref, l1_ref, l2_ref, co_ref,
         wo_ref, g2_ref, wu_ref, wd_ref, gf_ref, y_ref) = refs
        slabs = lambda ref: jnp.concatenate([ref[0, 0], ref[0, 1]], axis=1)
        ls = [slabs(l_ref) for l_ref in (l0_ref, l1_ref, l2_ref)]
        top = jnp.maximum(jnp.maximum(ls[0], ls[1]), ls[2])
        es = [jnp.exp(l - top) for l in ls]
        den = es[0] + es[1] + es[2]
        parts = [(slabs(o_ref) * (e / den)).astype(BF16) for o_ref, e in zip((o0_ref, o1_ref, o2_ref), es)]
        mixed = jnp.concatenate(parts + [co_ref[0]], axis=-1)
    else:
        x_ref, o_ref, co_ref, wo_ref, g2_ref, wu_ref, wd_ref, gf_ref, y_ref = refs
        mixed = jnp.concatenate([o_ref[0].astype(BF16), co_ref[0]], axis=-1)

    x1 = x_ref[0] + jnp.dot(mixed, wo_ref[...], preferred_element_type=F32)
    xn = _rmsnorm(x1, g2_ref[...]).astype(BF16)
    chunk = 1024
    n_chunks = wu_ref.shape[1] // chunk

    def up(c):
        hid = jnp.dot(xn, wu_ref[:, c * chunk:(c + 1) * chunk], preferred_element_type=F32)
        return jnp.square(jnp.maximum(hid, 0.0)).astype(BF16)

    acc = jnp.zeros_like(x1)
    hid_next = up(0)
    for c in range(n_chunks):
        hid = hid_next
        if c + 1 < n_chunks:
            hid_next = up(c + 1)
        acc = acc + jnp.dot(hid, wd_ref[c * chunk:(c + 1) * chunk, :], preferred_element_type=F32)
    y_ref[0] = _rmsnorm(x1 + acc, gf_ref[...])


def _post(x, attn_inputs, co, w_out, g2, w_up, w_down, gf, combine):
    b, t, d_model = x.shape
    tm = min(t, POST_TM)
    assert t % tm == 0

    def tok(a):
        if a.ndim == 4:
            return pl.BlockSpec((1, a.shape[1], tm, a.shape[3]), lambda bi, i: (bi, 0, i, 0))
        return pl.BlockSpec((1, tm, a.shape[2]), lambda bi, i: (bi, i, 0))

    const = lambda a: pl.BlockSpec(a.shape, lambda bi, i: (0, 0))
    return pl.pallas_call(
        functools.partial(_post_kernel, tm=tm, combine=combine),
        grid=(b, t // tm),
        in_specs=[tok(x)] + [tok(a) for a in attn_inputs]
                 + [tok(co), const(w_out), const(g2), const(w_up), const(w_down), const(gf)],
        out_specs=tok(x),
        out_shape=jax.ShapeDtypeStruct((b, t, d_model), F32),
        compiler_params=pltpu.CompilerParams(
            dimension_semantics=("arbitrary", "arbitrary"), vmem_limit_bytes=VMEM_LIMIT),
        name="post_prompt" if combine else "post_sample",
    )(x, *attn_inputs, co, w_out, g2, w_up, w_down, gf)


def kernel(x_prompt, x_sample, cache_kv_w128, cache_kv_w512, cache_kv_w2048, state_conv, norm_attn_g, w_in, conv_w, w_out, norm_mlp_g, w_up, w_down, norm_final_g):
    b, t, d_model = x_prompt.shape
    nb, steps, _ = x_sample.shape
    n_s = nb * steps
    assert w_in.shape[0] == 1, "single trunk layer"
    g1 = norm_attn_g[0][None, :]
    g2 = norm_mlp_g[0][None, :]
    gf = norm_final_g[None, :]
    w_in_b = w_in[0].astype(BF16)
    w_out_b = w_out[0].astype(BF16)
    w_up_b = w_up[0].astype(BF16)
    w_down_b = w_down[0].astype(BF16)
    cw = conv_w[0]

    caches = [jnp.transpose(c[0], (0, 2, 3, 4, 1)).reshape(nb, KV_W, w)
              for c, w in zip((cache_kv_w128, cache_kv_w512, cache_kv_w2048), WINDOWS)]
    state_pad = jnp.pad(state_conv[0], ((0, 0), (0, steps - (CONV_K - 1)), (0, 0))).reshape(n_s, CONV_CH)
    q_s, kn, vn, co_s, u_s = _proj_sample(x_sample.reshape(n_s, d_model), g1, w_in_b, cw, state_pad)
    r3 = lambda a: a.reshape(nb, steps, ATTN_W)
    o_s, n0, n1, n2 = _attn_sample(r3(q_s), r3(kn), r3(vn), caches)
    y_s = _post(x_sample.reshape(1, n_s, d_model), [o_s.reshape(1, n_s, ATTN_W)], co_s.reshape(1, n_s, CONV_CH),
                w_out_b, g2, w_up_b, w_down_b, gf, False)

    qkv0, qkv1, qkv2, co, kv0, kv1, kv2, ctail = _proj_prompt(x_prompt, g1, w_in_b, cw)
    attn = [_attn_prompt(qkv, g) for g, qkv in enumerate((qkv0, qkv1, qkv2))]
    attn_inputs = [o for o, _ in attn] + [l for _, l in attn]
    y_p = _post(x_prompt, attn_inputs, co, w_out_b, g2, w_up_b, w_down_b, gf, True)

    kv_p = lambda a, w: a.reshape(1, b, w, 2, HEADS, HEAD_DIM)
    kv_s = lambda a, w: jnp.transpose(a.reshape(nb, 2, HEADS, HEAD_DIM, w), (0, 4, 1, 2, 3))[None]
    return (y_p,
            y_s.reshape(nb, steps, d_model),
            kv_p(kv0, WINDOWS[0]),
            kv_p(kv1, WINDOWS[1]),
            kv_p(kv2, WINDOWS[2]),
            ctail[:, SUBLANES - (CONV_K - 1):, :][None],
            kv_s(n0, WINDOWS[0]),
            kv_s(n1, WINDOWS[1]),
            kv_s(n2, WINDOWS[2]),
            u_s.reshape(nb, steps, CONV_CH)[:, steps - (CONV_K - 1):, :][None])
```

```python
import functools

import jax
import jax.numpy as jnp
from jax import lax
from jax.experimental import pallas as pl
from jax.experimental.pallas import tpu as pltpu

F32 = jnp.float32
BF16 = jnp.bfloat16

EPS = 1e-6
N_GROUPS = 3
HEADS = 4
HEAD_DIM = 64
GROUP_W = HEADS * HEAD_DIM
ATTN_W = N_GROUPS * GROUP_W
QKV_W = 3 * GROUP_W
CONV_CH = 256
CONV_K = 3
KV_W = 2 * GROUP_W
WINDOWS = (128, 512, 2048)
DILATIONS = (1, 4, 16)
SPAN = 128
QK_SCALE = HEAD_DIM ** -0.5

_Q0, _K0, _V0, _B0, _C0, _H0, _END = 0, 768, 1536, 2304, 2560, 2816, 3072

SUBLANES = 8
LANES = 128
VMEM_LIMIT = 56 * 1024 * 1024

PROJ_TM = 512
POST_TM = 512
ATTN_SB = 2048
TILE = 128
NT_DIMS = (((1,), (1,)), ((), ()))


def _rmsnorm(x, g):
    y = x * lax.rsqrt(jnp.mean(x * x, axis=-1, keepdims=True) + EPS)
    return y * g


def _head_mask(h, dtype):
    lane = lax.broadcasted_iota(jnp.int32, (1, GROUP_W), 1)
    return ((lane >= h * HEAD_DIM) & (lane < (h + 1) * HEAD_DIM)).astype(F32).astype(dtype)


def _in_head(lane, h):
    return (lane >= h * HEAD_DIM) & (lane < (h + 1) * HEAD_DIM)


def _gated_conv(gate_b, u, prev1, prev2, cw_ref):
    y = cw_ref[0:1, :] * prev2
    y = y + cw_ref[1:2, :] * prev1
    y = y + cw_ref[2:3, :] * u
    return gate_b * y


def _proj_prompt_kernel(x_ref, g_ref, w_ref, cw_ref, qs_ref, kn_ref, vn_ref, c0_ref, c1_ref, c2_ref,
                        qkv0_ref, qkv1_ref, qkv2_ref, co_ref, kv0_ref, kv1_ref, kv2_ref, ct_ref,
                        os_ref, n0_ref, n1_ref, n2_ref,
                        carry_ref, slab_ref, *, tm, steps):
    i = pl.program_id(1)

    @pl.when(i == 0)
    def _():
        carry_ref[...] = jnp.zeros_like(carry_ref)

    c_refs = (c0_ref, c1_ref, c2_ref)
    sample = [_sample_scores(bb, qs_ref, kn_ref, vn_ref, c_refs, (n0_ref, n1_ref, n2_ref), steps)
              for bb in range(qs_ref.shape[0])]

    xn = _rmsnorm(x_ref[0], g_ref[...]).astype(BF16)

    def proj(lo, hi):
        return jnp.dot(xn, w_ref[:, lo:hi], preferred_element_type=F32)

    qf = proj(_Q0, _K0) * QK_SCALE
    kf = proj(_K0, _V0)
    vf = proj(_V0, _B0)
    for g, (qkv_ref, kv_ref) in enumerate(((qkv0_ref, kv0_ref), (qkv1_ref, kv1_ref), (qkv2_ref, kv2_ref))):
        cols = slice(g * GROUP_W, (g + 1) * GROUP_W)
        rows = kv_ref.shape[1]
        kv_ref[0, :, 0:GROUP_W] = kf[tm - rows:tm, cols]
        kv_ref[0, :, GROUP_W:KV_W] = vf[tm - rows:tm, cols]
        d = DILATIONS[g]
        piece = jnp.concatenate([qf[:, cols], kf[:, cols], vf[:, cols]], axis=1)
        if d == 1:
            qkv_ref[0, 0] = piece.astype(BF16)
        else:
            slabs = slab_ref.at[g - 1]
            for s in range(QKV_W // LANES):
                slabs[s] = piece[:, s * LANES:(s + 1) * LANES]
            for r in range(d):
                for s in range(QKV_W // LANES):
                    qkv_ref[0, r, :, s * LANES:(s + 1) * LANES] = (
                        slabs[s, pl.ds(r, tm // d, stride=d), :].astype(BF16))

    for bb, parts in enumerate(sample):
        _sample_values(bb, parts, c_refs, os_ref, steps)

    gate_b = proj(_B0, _C0)
    u = proj(_C0, _H0) * proj(_H0, _END)
    row = lax.broadcasted_iota(jnp.int32, (tm, CONV_CH), 0)
    carry = carry_ref[...]
    prev1 = jnp.where(row == 0, carry[7:8, :], pltpu.roll(u, 1, 0))
    prev2 = jnp.where(row == 0, carry[6:7, :],
                      jnp.where(row == 1, carry[7:8, :], pltpu.roll(u, 2, 0)))
    co_ref[0] = _gated_conv(gate_b, u, prev1, prev2, cw_ref).astype(BF16)
    tail = u[tm - SUBLANES:tm, :]
    carry_ref[...] = tail
    ct_ref[0] = tail


def _proj_prompt(x, g, w_in, conv_w, q_s, kn_s, vn_s, caches):
    b, t, d_model = x.shape
    tm = PROJ_TM
    nt = t // tm
    assert t % tm == 0 and t >= WINDOWS[-1]
    nb, steps, _ = q_s.shape
    bps = nb // (b * nt)
    assert bps * b * nt == nb and steps == SUBLANES
    assert all(c.shape[1:] == (KV_W, w) for c, w in zip(caches, WINDOWS))
    stok = pl.BlockSpec((bps, steps, ATTN_W), lambda bi, i: (bi * nt + i, 0, 0))
    cspecs = [pl.BlockSpec((bps, KV_W, w), lambda bi, i: (bi * nt + i, 0, 0)) for w in WINDOWS]

    def tail_spec(w):
        rows = min(w, tm)
        nblk = max(w // tm, 1)
        return (pl.BlockSpec((1, rows, KV_W), lambda bi, i: (bi, jnp.maximum(i - (nt - nblk), 0), 0)),
                jax.ShapeDtypeStruct((b, w, KV_W), F32))

    tails = [tail_spec(w) for w in WINDOWS]
    tok = lambda width: pl.BlockSpec((1, tm, width), lambda bi, i: (bi, i, 0))
    const = lambda shape: pl.BlockSpec(shape, lambda bi, i: (0, 0))
    qkv_specs = [pl.BlockSpec((1, d, tm // d, QKV_W), lambda bi, i: (bi, 0, i, 0)) for d in DILATIONS]
    qkv_shapes = [jax.ShapeDtypeStruct((b, d, t // d, QKV_W), BF16) for d in DILATIONS]
    return pl.pallas_call(
        functools.partial(_proj_prompt_kernel, tm=tm, steps=steps),
        grid=(b, nt),
        in_specs=[tok(d_model), const((1, d_model)), const((d_model, _END)), const((CONV_K, CONV_CH)),
                  stok, stok, stok] + cspecs,
        out_specs=qkv_specs + [tok(CONV_CH)] + [s for s, _ in tails]
                  + [pl.BlockSpec((1, SUBLANES, CONV_CH), lambda bi, i: (bi, 0, 0)), stok] + cspecs,
        out_shape=qkv_shapes + [jax.ShapeDtypeStruct((b, t, CONV_CH), BF16)] + [s for _, s in tails]
                  + [jax.ShapeDtypeStruct((b, SUBLANES, CONV_CH), F32),
                     jax.ShapeDtypeStruct((nb, steps, ATTN_W), F32)]
                  + [jax.ShapeDtypeStruct((nb, KV_W, w), F32) for w in WINDOWS],
        scratch_shapes=[pltpu.VMEM((SUBLANES, CONV_CH), F32),
                        pltpu.VMEM((N_GROUPS - 1, QKV_W // LANES, tm, LANES), F32)],
        compiler_params=pltpu.CompilerParams(
            dimension_semantics=("arbitrary", "arbitrary"), vmem_limit_bytes=VMEM_LIMIT),
        name="proj_prompt",
    )(x, g, w_in, conv_w, q_s, kn_s, vn_s, *caches)


def _proj_sample_kernel(x_ref, g_ref, w_ref, cw_ref, sp_ref,
                        q_ref, kn_ref, vn_ref, co_ref, u_ref, *, tm):
    xn = _rmsnorm(x_ref[...], g_ref[...]).astype(BF16)

    def proj(lo, hi):
        return jnp.dot(xn, w_ref[:, lo:hi], preferred_element_type=F32)

    q_ref[...] = proj(_Q0, _K0) * QK_SCALE
    kn_ref[...] = proj(_K0, _V0)
    vn_ref[...] = proj(_V0, _B0)
    gate_b = proj(_B0, _C0)
    u = proj(_C0, _H0) * proj(_H0, _END)
    step = lax.broadcasted_iota(jnp.int32, (tm, CONV_CH), 0) & (SUBLANES - 1)
    sp = sp_ref[...]
    prev1 = jnp.where(step == 0, pltpu.roll(sp, tm - 1, 0), pltpu.roll(u, 1, 0))
    prev2 = jnp.where(step < 2, sp, pltpu.roll(u, 2, 0))
    co_ref[...] = _gated_conv(gate_b, u, prev1, prev2, cw_ref).astype(BF16)
    u_ref[...] = u


def _proj_sample(x, g, w_in, conv_w, state_pad):
    n, d_model = x.shape
    tm = min(n, POST_TM)
    assert n % tm == 0 and tm % SUBLANES == 0
    tok = lambda width: pl.BlockSpec((tm, width), lambda i: (i, 0))
    const = lambda shape: pl.BlockSpec(shape, lambda i: (0, 0))
    return pl.pallas_call(
        functools.partial(_proj_sample_kernel, tm=tm),
        grid=(n // tm,),
        in_specs=[tok(d_model), const((1, d_model)), const((d_model, _END)), const((CONV_K, CONV_CH)),
                  tok(CONV_CH)],
        out_specs=[tok(ATTN_W), tok(ATTN_W), tok(ATTN_W), tok(CONV_CH), tok(CONV_CH)],
        out_shape=[jax.ShapeDtypeStruct((n, ATTN_W), F32)] * 3
                  + [jax.ShapeDtypeStruct((n, CONV_CH), BF16), jax.ShapeDtypeStruct((n, CONV_CH), F32)],
        compiler_params=pltpu.CompilerParams(
            dimension_semantics=("arbitrary",), vmem_limit_bytes=VMEM_LIMIT),
        name="proj_sample",
    )(x, g, w_in, conv_w, state_pad)


def _attn_prompt_kernel(qkv_ref, kp_ref, vp_ref, o_ref, l_ref, *, d, n_tiles):
    sb = pl.program_id(1)

    row = lax.broadcasted_iota(jnp.int32, (TILE, TILE + SPAN), 0)
    col = lax.broadcasted_iota(jnp.int32, (TILE, TILE + SPAN), 1)
    band = (col >= row) & (col <= row + SPAN)
    band_first = band & (col + jnp.where(sb == 0, 0, SPAN) >= SPAN)
    lane256 = lax.broadcasted_iota(jnp.int32, (TILE, GROUP_W), 1)
    k_cols = slice(GROUP_W, 2 * GROUP_W)
    v_cols = slice(2 * GROUP_W, 3 * GROUP_W)

    def keys_values(r, j, cols, prev_ref):
        if j == 0:
            return jnp.concatenate([prev_ref[0, r], qkv_ref[0, r, 0:TILE, cols]], axis=0)
        return qkv_ref[0, r, (j - 1) * TILE:(j + 1) * TILE, cols]

    def scores(r, j):
        q = qkv_ref[0, r, j * TILE:(j + 1) * TILE, 0:GROUP_W]
        kk = keys_values(r, j, k_cols, kp_ref)
        return [lax.dot_general(q * _head_mask(h, BF16), kk, NT_DIMS, preferred_element_type=F32)
                for h in range(HEADS)]

    def finish(r, j, ss):
        vv = keys_values(r, j, v_cols, vp_ref)
        valid = band_first if j == 0 else band
        o_acc = jnp.zeros((TILE, GROUP_W), F32)
        l_acc = jnp.zeros((TILE, GROUP_W), F32)
        for h in range(HEADS):
            s = jnp.where(valid, ss[h], -jnp.inf)
            m = jnp.max(s, axis=-1, keepdims=True)
            p = jnp.exp(s - m)
            z = jnp.sum(p, axis=-1, keepdims=True)
            of = jnp.dot(p.astype(BF16), vv, preferred_element_type=F32)
            o_acc = jnp.where(_in_head(lane256, h), of / z, o_acc)
            l_acc = jnp.where(_in_head(lane256, h), m + jnp.log(z), l_acc)
        if d == 1:
            rows = pl.ds(j * TILE, TILE)
        else:
            rows = pl.ds(j * TILE * d + r, TILE, stride=d)
        for half in range(GROUP_W // LANES):
            lanes = slice(half * LANES, (half + 1) * LANES)
            o_ref[0, half, rows, :] = o_acc[:, lanes]
            l_ref[0, half, rows, :] = l_acc[:, lanes]

    units = [(r, j) for r in range(d) for j in range(n_tiles)]
    ss_next = scores(*units[0])
    for n, unit in enumerate(units):
        ss = ss_next
        if n + 1 < len(units):
            ss_next = scores(*units[n + 1])
        finish(*unit, ss)


def _attn_prompt(qkv, g):
    b, d, tc, _ = qkv.shape
    t = tc * d
    sb = min(ATTN_SB, t)
    n_tiles = sb // d // TILE
    assert d == DILATIONS[g] and t % sb == 0 and sb % (d * TILE) == 0 and TILE == SPAN

    def prev_spec(col_block):
        return pl.BlockSpec((1, d, SPAN, GROUP_W),
                            lambda bi, s: (bi, 0, jnp.maximum(s * n_tiles - 1, 0), col_block))

    return pl.pallas_call(
        functools.partial(_attn_prompt_kernel, d=d, n_tiles=n_tiles),
        grid=(b, t // sb),
        in_specs=[pl.BlockSpec((1, d, sb // d, QKV_W), lambda bi, s: (bi, 0, s, 0)),
                  prev_spec(1), prev_spec(2)],
        out_specs=[pl.BlockSpec((1, 2, sb, LANES), lambda bi, s: (bi, 0, s, 0))] * 2,
        out_shape=[jax.ShapeDtypeStruct((b, 2, t, LANES), F32)] * 2,
        compiler_params=pltpu.CompilerParams(
            dimension_semantics=("arbitrary", "arbitrary"), vmem_limit_bytes=VMEM_LIMIT),
        name=f"attn_prompt_g{g}",
    )(qkv, qkv, qkv)


def _sample_scores(bb, q_ref, kn_ref, vn_ref, c_refs, n_refs, steps):
    rows = HEADS * steps
    step_of = lambda shape: lax.broadcasted_iota(jnp.int32, shape, 0) & (steps - 1)
    lane_kv = lax.broadcasted_iota(jnp.int32, (KV_W, LANES), 1)
    pad = jnp.zeros((LANES - steps, GROUP_W), F32)
    parts = []
    for g, (c_ref, n_ref) in enumerate(zip(c_refs, n_refs)):
        w = c_ref.shape[2]
        d = DILATIONS[g]
        kn = kn_ref[bb, :, g * GROUP_W:(g + 1) * GROUP_W]
        vn = vn_ref[bb, :, g * GROUP_W:(g + 1) * GROUP_W]
        knt = jnp.concatenate([kn, pad], axis=0).T
        vnt = jnp.concatenate([vn, pad], axis=0).T
        shift = LANES - steps
        nxt = pltpu.roll(c_ref[bb, :, 0:LANES], shift, 1)
        for cb in range(w // LANES):
            cur = nxt
            if cb + 1 < w // LANES:
                nxt = pltpu.roll(c_ref[bb, :, (cb + 1) * LANES:(cb + 2) * LANES], shift, 1)
            else:
                nxt = pltpu.roll(jnp.concatenate([knt, vnt], axis=0), shift, 1)
            n_ref[bb, :, cb * LANES:(cb + 1) * LANES] = jnp.where(lane_kv < shift, cur, nxt)

        qg = q_ref[bb, :, g * GROUP_W:(g + 1) * GROUP_W]
        qs = jnp.concatenate([qg * _head_mask(h, F32) for h in range(HEADS)], axis=0).astype(BF16)
        ck = c_ref[bb, 0:GROUP_W, :].astype(BF16)
        s_c = jnp.dot(qs, ck, preferred_element_type=F32)
        s_n = jnp.dot(qs, knt.astype(BF16), preferred_element_type=F32)
        t_c = step_of((rows, w))
        i_c = lax.broadcasted_iota(jnp.int32, (rows, w), 1)
        valid_c = (i_c >= t_c) & (((i_c - t_c) & (d - 1)) == 0)
        t_n = step_of((rows, LANES))
        i_n = lax.broadcasted_iota(jnp.int32, (rows, LANES), 1)
        valid_n = (i_n <= t_n) & (((t_n - i_n) & (d - 1)) == 0)
        s_c = jnp.where(valid_c, s_c, -jnp.inf)
        s_n = jnp.where(valid_n, s_n, -jnp.inf)
        m = jnp.maximum(jnp.max(s_c, axis=-1, keepdims=True), jnp.max(s_n, axis=-1, keepdims=True))
        p_c = jnp.exp(s_c - m)
        p_n = jnp.exp(s_n - m)
        z = jnp.sum(p_c, axis=-1, keepdims=True) + jnp.sum(p_n, axis=-1, keepdims=True)
        parts.append((p_c.astype(BF16), p_n.astype(BF16), z, m, vnt.astype(BF16)))
    return parts


def _sample_values(bb, parts, c_refs, o_ref, steps):
    lane256 = lax.broadcasted_iota(jnp.int32, (steps, GROUP_W), 1)
    outs, lses = [], []
    for (p_c, p_n, z, m, vnt), c_ref in zip(parts, c_refs):
        cv = c_ref[bb, GROUP_W:KV_W, :].astype(BF16)
        of = lax.dot_general(p_c, cv, NT_DIMS, preferred_element_type=F32)
        of = of + lax.dot_general(p_n, vnt, NT_DIMS, preferred_element_type=F32)
        outs.append(of / z)
        lses.append(m + jnp.log(z))
    top = jnp.maximum(jnp.maximum(lses[0], lses[1]), lses[2])
    es = [jnp.exp(l - top) for l in lses]
    den = es[0] + es[1] + es[2]
    for g in range(N_GROUPS):
        og = outs[g] * (es[g] / den)
        res = jnp.zeros((steps, GROUP_W), F32)
        for h in range(HEADS):
            res = jnp.where(_in_head(lane256, h), og[h * steps:(h + 1) * steps, :], res)
        o_ref[bb, :, g * GROUP_W:(g + 1) * GROUP_W] = res


def _post_kernel(*refs, tm, combine):
    if combine:
        (x_ref, o0_ref, o1_ref, o2_ref, l0_ref, l1_ref, l2_ref, co_ref,
         wo_ref, g2_ref, wu_ref, wd_ref, gf_ref, y_ref) = refs
        slabs = lambda ref: jnp.concatenate([ref[0, 0], ref[0, 1]], axis=1)
        ls = [slabs(l_ref) for l_ref in (l0_ref, l1_ref, l2_ref)]
        top = jnp.maximum(jnp.maximum(ls[0], ls[1]), ls[2])
        es = [jnp.exp(l - top) for l in ls]
        den = es[0] + es[1] + es[2]
        parts = [(slabs(o_ref) * (e / den)).astype(BF16) for o_ref, e in zip((o0_ref, o1_ref, o2_ref), es)]
        mixed = jnp.concatenate(parts + [co_ref[0]], axis=-1)
    else:
        x_ref, o_ref, co_ref, wo_ref, g2_ref, wu_ref, wd_ref, gf_ref, y_ref = refs
        mixed = jnp.concatenate([o_ref[0].astype(BF16), co_ref[0]], axis=-1)

    x1 = x_ref[0] + jnp.dot(mixed, wo_ref[...], preferred_element_type=F32)
    xn = _rmsnorm(x1, g2_ref[...]).astype(BF16)
    chunk = 1024
    n_chunks = wu_ref.shape[1] // chunk

    def up(c):
        hid = jnp.dot(xn, wu_ref[:, c * chunk:(c + 1) * chunk], preferred_element_type=F32)
        return jnp.square(jnp.maximum(hid, 0.0)).astype(BF16)

    acc = jnp.zeros_like(x1)
    hid_next = up(0)
    for c in range(n_chunks):
        hid = hid_next
        if c + 1 < n_chunks:
            hid_next = up(c + 1)
        acc = acc + jnp.dot(hid, wd_ref[c * chunk:(c + 1) * chunk, :], preferred_element_type=F32)
    y_ref[0] = _rmsnorm(x1 + acc, gf_ref[...])


def _post(x, attn_inputs, co, w_out, g2, w_up, w_down, gf, combine):
    b, t, d_model = x.shape
    tm = min(t, POST_TM)
    assert t % tm == 0

    def tok(a):
        if a.ndim == 4:
            return pl.BlockSpec((1, a.shape[1], tm, a.shape[3]), lambda bi, i: (bi, 0, i, 0))
        return pl.BlockSpec((1, tm, a.shape[2]), lambda bi, i: (bi, i, 0))

    const = lambda a: pl.BlockSpec(a.shape, lambda bi, i: (0, 0))
    return pl.pallas_call(
        functools.partial(_post_kernel, tm=tm, combine=combine),
        grid=(b, t // tm),
        in_specs=[tok(x)] + [tok(a) for a in attn_inputs]
                 + [tok(co), const(w_out), const(g2), const(w_up), const(w_down), const(gf)],
        out_specs=tok(x),
        out_shape=jax.ShapeDtypeStruct((b, t, d_model), F32),
        compiler_params=pltpu.CompilerParams(
            dimension_semantics=("arbitrary", "arbitrary"), vmem_limit_bytes=VMEM_LIMIT),
        name="post_prompt" if combine else "post_sample",
    )(x, *attn_inputs, co, w_out, g2, w_up, w_down, gf)


def kernel(x_prompt, x_sample, cache_kv_w128, cache_kv_w512, cache_kv_w2048, state_conv, norm_attn_g, w_in, conv_w, w_out, norm_mlp_g, w_up, w_down, norm_final_g):
    b, t, d_model = x_prompt.shape
    nb, steps, _ = x_sample.shape
    n_s = nb * steps
    assert w_in.shape[0] == 1, "single trunk layer"
    g1 = norm_attn_g[0][None, :]
    g2 = norm_mlp_g[0][None, :]
    gf = norm_final_g[None, :]
    w_in_b = w_in[0].astype(BF16)
    w_out_b = w_out[0].astype(BF16)
    w_up_b = w_up[0].astype(BF16)
    w_down_b = w_down[0].astype(BF16)
    cw = conv_w[0]

    caches = [jnp.transpose(c[0], (0, 2, 3, 4, 1)).reshape(nb, KV_W, w)
              for c, w in zip((cache_kv_w128, cache_kv_w512, cache_kv_w2048), WINDOWS)]
    state_pad = jnp.pad(state_conv[0], ((0, 0), (0, steps - (CONV_K - 1)), (0, 0))).reshape(n_s, CONV_CH)
    q_s, kn, vn, co_s, u_s = _proj_sample(x_sample.reshape(n_s, d_model), g1, w_in_b, cw, state_pad)
    r3 = lambda a: a.reshape(nb, steps, ATTN_W)
    qkv0, qkv1, qkv2, co, kv0, kv1, kv2, ctail, o_s, n0, n1, n2 = _proj_prompt(
        x_prompt, g1, w_in_b, cw, r3(q_s), r3(kn), r3(vn), caches)
    y_s = _post(x_sample.reshape(1, n_s, d_model), [o_s.reshape(1, n_s, ATTN_W)], co_s.reshape(1, n_s, CONV_CH),
                w_out_b, g2, w_up_b, w_down_b, gf, False)

    attn = [_attn_prompt(qkv, g) for g, qkv in enumerate((qkv0, qkv1, qkv2))]
    attn_inputs = [o for o, _ in attn] + [l for _, l in attn]
    y_p = _post(x_prompt, attn_inputs, co, w_out_b, g2, w_up_b, w_down_b, gf, True)

    kv_p = lambda a, w: a.reshape(1, b, w, 2, HEADS, HEAD_DIM)
    kv_s = lambda a, w: jnp.transpose(a.reshape(nb, 2, HEADS, HEAD_DIM, w), (0, 4, 1, 2, 3))[None]
    return (y_p,
            y_s.reshape(nb, steps, d_model),
            kv_p(kv0, WINDOWS[0]),
            kv_p(kv1, WINDOWS[1]),
            kv_p(kv2, WINDOWS[2]),
            ctail[:, SUBLANES - (CONV_K - 1):, :][None],
            kv_s(n0, WINDOWS[0]),
            kv_s(n1, WINDOWS[1]),
            kv_s(n2, WINDOWS[2]),
            u_s.reshape(nb, steps, CONV_CH)[:, steps - (CONV_K - 1):, :][None])
```

```python
import functools

import jax
import jax.numpy as jnp
from jax import lax
from jax.experimental import pallas as pl
from jax.experimental.pallas import tpu as pltpu

F32 = jnp.float32
BF16 = jnp.bfloat16

EPS = 1e-6
N_GROUPS = 3
HEADS = 4
HEAD_DIM = 64
GROUP_W = HEADS * HEAD_DIM
ATTN_W = N_GROUPS * GROUP_W
QKV_W = 3 * GROUP_W
CONV_CH = 256
CONV_K = 3
KV_W = 2 * GROUP_W
WINDOWS = (128, 512, 2048)
DILATIONS = (1, 4, 16)
SPAN = 128
QK_SCALE = HEAD_DIM ** -0.5
LOG2E = 1.4426950408889634
LN2 = 0.6931471805599453

_Q0, _K0, _V0, _B0, _C0, _H0, _END = 0, 768, 1536, 2304, 2560, 2816, 3072

SUBLANES = 8
LANES = 128
VMEM_LIMIT = 56 * 1024 * 1024

PROJ_TM = 512
POST_TM = 512
ATTN_SB = 2048
TILE = 128
NT_DIMS = (((1,), (1,)), ((), ()))


def _rmsnorm(x, g):
    y = x * lax.rsqrt(jnp.mean(x * x, axis=-1, keepdims=True) + EPS)
    return y * g


def _head_mask(h, dtype):
    lane = lax.broadcasted_iota(jnp.int32, (1, GROUP_W), 1)
    return ((lane >= h * HEAD_DIM) & (lane < (h + 1) * HEAD_DIM)).astype(F32).astype(dtype)


def _in_head(lane, h):
    return (lane >= h * HEAD_DIM) & (lane < (h + 1) * HEAD_DIM)


def _gated_conv(gate_b, u, prev1, prev2, cw_ref):
    y = cw_ref[0:1, :] * prev2
    y = y + cw_ref[1:2, :] * prev1
    y = y + cw_ref[2:3, :] * u
    return gate_b * y


def _proj_prompt_kernel(x_ref, g_ref, w_ref, cw_ref, qs_ref, kn_ref, vn_ref, c0_ref, c1_ref, c2_ref,
                        qkv0_ref, qkv1_ref, qkv2_ref, co_ref, kv0_ref, kv1_ref, kv2_ref, ct_ref,
                        os_ref, n0_ref, n1_ref, n2_ref,
                        carry_ref, slab_ref, *, tm, steps):
    i = pl.program_id(1)

    @pl.when(i == 0)
    def _():
        carry_ref[...] = jnp.zeros_like(carry_ref)

    c_refs = (c0_ref, c1_ref, c2_ref)
    sample = [_sample_scores(bb, qs_ref, kn_ref, vn_ref, c_refs, (n0_ref, n1_ref, n2_ref), steps)
              for bb in range(qs_ref.shape[0])]

    xn = _rmsnorm(x_ref[0], g_ref[...]).astype(BF16)

    def proj(lo, hi):
        return jnp.dot(xn, w_ref[:, lo:hi], preferred_element_type=F32)

    qf = proj(_Q0, _K0) * (QK_SCALE * LOG2E)
    kf = proj(_K0, _V0)
    vf = proj(_V0, _B0)
    for g, (qkv_ref, kv_ref) in enumerate(((qkv0_ref, kv0_ref), (qkv1_ref, kv1_ref), (qkv2_ref, kv2_ref))):
        cols = slice(g * GROUP_W, (g + 1) * GROUP_W)
        rows = kv_ref.shape[1]
        kv_ref[0, :, 0:GROUP_W] = kf[tm - rows:tm, cols]
        kv_ref[0, :, GROUP_W:KV_W] = vf[tm - rows:tm, cols]
        d = DILATIONS[g]
        piece = jnp.concatenate([qf[:, cols], kf[:, cols], vf[:, cols]], axis=1)
        if d == 1:
            qkv_ref[0, 0] = piece.astype(BF16)
        else:
            slabs = slab_ref.at[g - 1]
            for s in range(QKV_W // LANES):
                slabs[s] = piece[:, s * LANES:(s + 1) * LANES]
            for r in range(d):
                for s in range(QKV_W // LANES):
                    qkv_ref[0, r, :, s * LANES:(s + 1) * LANES] = (
                        slabs[s, pl.ds(r, tm // d, stride=d), :].astype(BF16))

    for bb, parts in enumerate(sample):
        _sample_values(bb, parts, c_refs, os_ref, steps)

    gate_b = proj(_B0, _C0)
    u = proj(_C0, _H0) * proj(_H0, _END)
    row = lax.broadcasted_iota(jnp.int32, (tm, CONV_CH), 0)
    carry = carry_ref[...]
    prev1 = jnp.where(row == 0, carry[7:8, :], pltpu.roll(u, 1, 0))
    prev2 = jnp.where(row == 0, carry[6:7, :],
                      jnp.where(row == 1, carry[7:8, :], pltpu.roll(u, 2, 0)))
    co_ref[0] = _gated_conv(gate_b, u, prev1, prev2, cw_ref).astype(BF16)
    tail = u[tm - SUBLANES:tm, :]
    carry_ref[...] = tail
    ct_ref[0] = tail


def _proj_prompt(x, g, w_in, conv_w, q_s, kn_s, vn_s, caches):
    b, t, d_model = x.shape
    tm = PROJ_TM
    nt = t // tm
    assert t % tm == 0 and t >= WINDOWS[-1]
    nb, steps, _ = q_s.shape
    bps = nb // (b * nt)
    assert bps * b * nt == nb and steps == SUBLANES
    assert all(c.shape[1:] == (KV_W, w) for c, w in zip(caches, WINDOWS))
    stok = pl.BlockSpec((bps, steps, ATTN_W), lambda bi, i: (bi * nt + i, 0, 0))
    cspecs = [pl.BlockSpec((bps, KV_W, w), lambda bi, i: (bi * nt + i, 0, 0)) for w in WINDOWS]

    def tail_spec(w):
        rows = min(w, tm)
        nblk = max(w // tm, 1)
        return (pl.BlockSpec((1, rows, KV_W), lambda bi, i: (bi, jnp.maximum(i - (nt - nblk), 0), 0)),
                jax.ShapeDtypeStruct((b, w, KV_W), F32))

    tails = [tail_spec(w) for w in WINDOWS]
    tok = lambda width: pl.BlockSpec((1, tm, width), lambda bi, i: (bi, i, 0))
    const = lambda shape: pl.BlockSpec(shape, lambda bi, i: (0, 0))
    qkv_specs = [pl.BlockSpec((1, d, tm // d, QKV_W), lambda bi, i: (bi, 0, i, 0)) for d in DILATIONS]
    qkv_shapes = [jax.ShapeDtypeStruct((b, d, t // d, QKV_W), BF16) for d in DILATIONS]
    return pl.pallas_call(
        functools.partial(_proj_prompt_kernel, tm=tm, steps=steps),
        grid=(b, nt),
        in_specs=[tok(d_model), const((1, d_model)), const((d_model, _END)), const((CONV_K, CONV_CH)),
                  stok, stok, stok] + cspecs,
        out_specs=qkv_specs + [tok(CONV_CH)] + [s for s, _ in tails]
                  + [pl.BlockSpec((1, SUBLANES, CONV_CH), lambda bi, i: (bi, 0, 0)), stok] + cspecs,
        out_shape=qkv_shapes + [jax.ShapeDtypeStruct((b, t, CONV_CH), BF16)] + [s for _, s in tails]
                  + [jax.ShapeDtypeStruct((b, SUBLANES, CONV_CH), F32),
                     jax.ShapeDtypeStruct((nb, steps, ATTN_W), F32)]
                  + [jax.ShapeDtypeStruct((nb, KV_W, w), F32) for w in WINDOWS],
        scratch_shapes=[pltpu.VMEM((SUBLANES, CONV_CH), F32),
                        pltpu.VMEM((N_GROUPS - 1, QKV_W // LANES, tm, LANES), F32)],
        compiler_params=pltpu.CompilerParams(
            dimension_semantics=("arbitrary", "arbitrary"), vmem_limit_bytes=VMEM_LIMIT),
        name="proj_prompt",
    )(x, g, w_in, conv_w, q_s, kn_s, vn_s, *caches)


def _proj_sample_kernel(x_ref, g_ref, w_ref, cw_ref, sp_ref,
                        q_ref, kn_ref, vn_ref, co_ref, u_ref, *, tm):
    xn = _rmsnorm(x_ref[...], g_ref[...]).astype(BF16)

    def proj(lo, hi):
        return jnp.dot(xn, w_ref[:, lo:hi], preferred_element_type=F32)

    q_ref[...] = proj(_Q0, _K0) * QK_SCALE
    kn_ref[...] = proj(_K0, _V0)
    vn_ref[...] = proj(_V0, _B0)
    gate_b = proj(_B0, _C0)
    u = proj(_C0, _H0) * proj(_H0, _END)
    step = lax.broadcasted_iota(jnp.int32, (tm, CONV_CH), 0) & (SUBLANES - 1)
    sp = sp_ref[...]
    prev1 = jnp.where(step == 0, pltpu.roll(sp, tm - 1, 0), pltpu.roll(u, 1, 0))
    prev2 = jnp.where(step < 2, sp, pltpu.roll(u, 2, 0))
    co_ref[...] = _gated_conv(gate_b, u, prev1, prev2, cw_ref).astype(BF16)
    u_ref[...] = u


def _proj_sample(x, g, w_in, conv_w, state_pad):
    n, d_model = x.shape
    tm = min(n, POST_TM)
    assert n % tm == 0 and tm % SUBLANES == 0
    tok = lambda width: pl.BlockSpec((tm, width), lambda i: (i, 0))
    const = lambda shape: pl.BlockSpec(shape, lambda i: (0, 0))
    return pl.pallas_call(
        functools.partial(_proj_sample_kernel, tm=tm),
        grid=(n // tm,),
        in_specs=[tok(d_model), const((1, d_model)), const((d_model, _END)), const((CONV_K, CONV_CH)),
                  tok(CONV_CH)],
        out_specs=[tok(ATTN_W), tok(ATTN_W), tok(ATTN_W), tok(CONV_CH), tok(CONV_CH)],
        out_shape=[jax.ShapeDtypeStruct((n, ATTN_W), F32)] * 3
                  + [jax.ShapeDtypeStruct((n, CONV_CH), BF16), jax.ShapeDtypeStruct((n, CONV_CH), F32)],
        compiler_params=pltpu.CompilerParams(
            dimension_semantics=("arbitrary",), vmem_limit_bytes=VMEM_LIMIT),
        name="proj_sample",
    )(x, g, w_in, conv_w, state_pad)


def _attn_prompt_kernel(qkv_ref, kp_ref, vp_ref, o_ref, l_ref, *, d, n_tiles):
    sb = pl.program_id(1)

    key = lax.broadcasted_iota(jnp.int32, (TILE + SPAN, TILE), 0)
    qry = lax.broadcasted_iota(jnp.int32, (TILE + SPAN, TILE), 1)
    band = (key >= qry) & (key <= qry + SPAN)
    band_first = band & (key + jnp.where(sb == 0, 0, SPAN) >= SPAN)
    k_cols = slice(GROUP_W, 2 * GROUP_W)
    v_cols = slice(2 * GROUP_W, 3 * GROUP_W)

    def keys_values(r, j, cols, prev_ref):
        if j == 0:
            return jnp.concatenate([prev_ref[0, r], qkv_ref[0, r, 0:TILE, cols]], axis=0)
        return qkv_ref[0, r, (j - 1) * TILE:(j + 1) * TILE, cols]

    def scores(r, j):
        q = qkv_ref[0, r, j * TILE:(j + 1) * TILE, 0:GROUP_W]
        kk = keys_values(r, j, k_cols, kp_ref)
        ss = []
        for h in range(0, HEADS, 2):
            q2 = jnp.concatenate([q * _head_mask(h, BF16), q * _head_mask(h + 1, BF16)], axis=0)
            s2 = lax.dot_general(kk, q2, NT_DIMS, preferred_element_type=F32)
            ss += [s2[:, 0:TILE], s2[:, TILE:2 * TILE]]
        return ss

    def finish(r, j, ss):
        vt = keys_values(r, j, v_cols, vp_ref).T
        valid = band_first if j == 0 else band
        o_t, l_t = [], []
        for h in range(HEADS):
            s = jnp.where(valid, ss[h], -jnp.inf)
            m = jnp.max(s, axis=0, keepdims=True)
            p = jnp.exp2(s - m)
            z = jnp.sum(p, axis=0, keepdims=True)
            acc = jnp.dot(vt[h * HEAD_DIM:(h + 1) * HEAD_DIM, :], p.astype(BF16),
                          preferred_element_type=F32)
            o_t.append(acc * (1.0 / z))
            l_t.append(jnp.broadcast_to(m * LN2 + jnp.log(z), (HEAD_DIM, TILE)))
        o_acc = jnp.concatenate(o_t, axis=0).T
        l_acc = jnp.concatenate(l_t, axis=0).T
        if d == 1:
            rows = pl.ds(j * TILE, TILE)
        else:
            rows = pl.ds(j * TILE * d + r, TILE, stride=d)
        for half in range(GROUP_W // LANES):
            lanes = slice(half * LANES, (half + 1) * LANES)
            o_ref[0, half, rows, :] = o_acc[:, lanes]
            l_ref[0, half, rows, :] = l_acc[:, lanes]

    units = [(r, j) for r in range(d) for j in range(n_tiles)]
    ss_next = scores(*units[0])
    for n, unit in enumerate(units):
        ss = ss_next
        if n + 1 < len(units):
            ss_next = scores(*units[n + 1])
        finish(*unit, ss)


def _attn_prompt(qkv, g):
    b, d, tc, _ = qkv.shape
    t = tc * d
    sb = min(ATTN_SB, t)
    n_tiles = sb // d // TILE
    assert d == DILATIONS[g] and t % sb == 0 and sb % (d * TILE) == 0 and TILE == SPAN

    def prev_spec(col_block):
        return pl.BlockSpec((1, d, SPAN, GROUP_W),
                            lambda bi, s: (bi, 0, jnp.maximum(s * n_tiles - 1, 0), col_block))

    return pl.pallas_call(
        functools.partial(_attn_prompt_kernel, d=d, n_tiles=n_tiles),
        grid=(b, t // sb),
        in_specs=[pl.BlockSpec((1, d, sb // d, QKV_W), lambda bi, s: (bi, 0, s, 0)),
                  prev_spec(1), prev_spec(2)],
        out_specs=[pl.BlockSpec((1, 2, sb, LANES), lambda bi, s: (bi, 0, s, 0))] * 2,
        out_shape=[jax.ShapeDtypeStruct((b, 2, t, LANES), F32)] * 2,
        compiler_params=pltpu.CompilerParams(
            dimension_semantics=("arbitrary", "arbitrary"), vmem_limit_bytes=VMEM_LIMIT),
        name=f"attn_prompt_g{g}",
    )(qkv, qkv, qkv)


def _sample_scores(bb, q_ref, kn_ref, vn_ref, c_refs, n_refs, steps):
    rows = HEADS * steps
    step_of = lambda shape: lax.broadcasted_iota(jnp.int32, shape, 0) & (steps - 1)
    lane_kv = lax.broadcasted_iota(jnp.int32, (KV_W, LANES), 1)
    pad = jnp.zeros((LANES - steps, GROUP_W), F32)
    parts = []
    for g, (c_ref, n_ref) in enumerate(zip(c_refs, n_refs)):
        w = c_ref.shape[2]
        d = DILATIONS[g]
        kn = kn_ref[bb, :, g * GROUP_W:(g + 1) * GROUP_W]
        vn = vn_ref[bb, :, g * GROUP_W:(g + 1) * GROUP_W]
        knt = jnp.concatenate([kn, pad], axis=0).T
        vnt = jnp.concatenate([vn, pad], axis=0).T
        shift = LANES - steps
        nxt = pltpu.roll(c_ref[bb, :, 0:LANES], shift, 1)
        for cb in range(w // LANES):
            cur = nxt
            if cb + 1 < w // LANES:
                nxt = pltpu.roll(c_ref[bb, :, (cb + 1) * LANES:(cb + 2) * LANES], shift, 1)
            else:
                nxt = pltpu.roll(jnp.concatenate([knt, vnt], axis=0), shift, 1)
            n_ref[bb, :, cb * LANES:(cb + 1) * LANES] = jnp.where(lane_kv < shift, cur, nxt)

        qg = q_ref[bb, :, g * GROUP_W:(g + 1) * GROUP_W]
        qs = jnp.concatenate([qg * _head_mask(h, F32) for h in range(HEADS)], axis=0).astype(BF16)
        ck = c_ref[bb, 0:GROUP_W, :].astype(BF16)
        s_c = jnp.dot(qs, ck, preferred_element_type=F32)
        s_n = jnp.dot(qs, knt.astype(BF16), preferred_element_type=F32)
        t_c = step_of((rows, w))
        i_c = lax.broadcasted_iota(jnp.int32, (rows, w), 1)
        valid_c = (i_c >= t_c) & (((i_c - t_c) & (d - 1)) == 0)
        t_n = step_of((rows, LANES))
        i_n = lax.broadcasted_iota(jnp.int32, (rows, LANES), 1)
        valid_n = (i_n <= t_n) & (((t_n - i_n) & (d - 1)) == 0)
        s_c = jnp.where(valid_c, s_c, -jnp.inf)
        s_n = jnp.where(valid_n, s_n, -jnp.inf)
        m = jnp.maximum(jnp.max(s_c, axis=-1, keepdims=True), jnp.max(s_n, axis=-1, keepdims=True))
        p_c = jnp.exp(s_c - m)
        p_n = jnp.exp(s_n - m)
        z = jnp.sum(p_c, axis=-1, keepdims=True) + jnp.sum(p_n, axis=-1, keepdims=True)
        parts.append((p_c.astype(BF16), p_n.astype(BF16), z, m, vnt.astype(BF16)))
    return parts


def _sample_values(bb, parts, c_refs, o_ref, steps):
    lane256 = lax.broadcasted_iota(jnp.int32, (steps, GROUP_W), 1)
    outs, lses = [], []
    for (p_c, p_n, z, m, vnt), c_ref in zip(parts, c_refs):
        cv = c_ref[bb, GROUP_W:KV_W, :].astype(BF16)
        of = lax.dot_general(p_c, cv, NT_DIMS, preferred_element_type=F32)
        of = of + lax.dot_general(p_n, vnt, NT_DIMS, preferred_element_type=F32)
        outs.append(of / z)
        lses.append(m + jnp.log(z))
    top = jnp.maximum(jnp.maximum(lses[0], lses[1]), lses[2])
    es = [jnp.exp(l - top) for l in lses]
    den = es[0] + es[1] + es[2]
    for g in range(N_GROUPS):
        og = outs[g] * (es[g] / den)
        res = jnp.zeros((steps, GROUP_W), F32)
        for h in range(HEADS):
            res = jnp.where(_in_head(lane256, h), og[h * steps:(h + 1) * steps, :], res)
        o_ref[bb, :, g * GROUP_W:(g + 1) * GROUP_W] = res


def _post_kernel(*refs, tm, combine):
    if combine:
        (x_ref, o0_ref, o1_ref, o2_ref, l0_ref, l1_ref, l2_ref, co_ref,
         wo_ref, g2_ref, wu_ref, wd_ref, gf_ref, y_ref) = refs
        slabs = lambda ref: jnp.concatenate([ref[0, 0], ref[0, 1]], axis=1)
        ls = [slabs(l_ref) for l_ref in (l0_ref, l1_ref, l2_ref)]
        top = jnp.maximum(jnp.maximum(ls[0], ls[1]), ls[2])
        es = [jnp.exp(l - top) for l in ls]
        den = es[0] + es[1] + es[2]
        parts = [(slabs(o_ref) * (e / den)).astype(BF16) for o_ref, e in zip((o0_ref, o1_ref, o2_ref), es)]
        mixed = jnp.concatenate(parts + [co_ref[0]], axis=-1)
    else:
        x_ref, o_ref, co_ref, wo_ref, g2_ref, wu_ref, wd_ref, gf_ref, y_ref = refs
        mixed = jnp.concatenate([o_ref[0].astype(BF16), co_ref[0]], axis=-1)

    x1 = x_ref[0] + jnp.dot(mixed, wo_ref[...], preferred_element_type=F32)
    xn = _rmsnorm(x1, g2_ref[...]).astype(BF16)
    chunk = 1024
    n_chunks = wu_ref.shape[1] // chunk

    def up(c):
        hid = jnp.dot(xn, wu_ref[:, c * chunk:(c + 1) * chunk], preferred_element_type=F32)
        return jnp.square(jnp.maximum(hid, 0.0)).astype(BF16)

    acc = jnp.zeros_like(x1)
    hid_next = up(0)
    for c in range(n_chunks):
        hid = hid_next
        if c + 1 < n_chunks:
            hid_next = up(c + 1)
        acc = acc + jnp.dot(hid, wd_ref[c * chunk:(c + 1) * chunk, :], preferred_element_type=F32)
    y_ref[0] = _rmsnorm(x1 + acc, gf_ref[...])


def _post(x, attn_inputs, co, w_out, g2, w_up, w_down, gf, combine):
    b, t, d_model = x.shape
    tm = min(t, POST_TM)
    assert t % tm == 0

    def tok(a):
        if a.ndim == 4:
            return pl.BlockSpec((1, a.shape[1], tm, a.shape[3]), lambda bi, i: (bi, 0, i, 0))
        return pl.BlockSpec((1, tm, a.shape[2]), lambda bi, i: (bi, i, 0))

    const = lambda a: pl.BlockSpec(a.shape, lambda bi, i: (0, 0))
    return pl.pallas_call(
        functools.partial(_post_kernel, tm=tm, combine=combine),
        grid=(b, t // tm),
        in_specs=[tok(x)] + [tok(a) for a in attn_inputs]
                 + [tok(co), const(w_out), const(g2), const(w_up), const(w_down), const(gf)],
        out_specs=tok(x),
        out_shape=jax.ShapeDtypeStruct((b, t, d_model), F32),
        compiler_params=pltpu.CompilerParams(
            dimension_semantics=("arbitrary", "arbitrary"), vmem_limit_bytes=VMEM_LIMIT),
        name="post_prompt" if combine else "post_sample",
    )(x, *attn_inputs, co, w_out, g2, w_up, w_down, gf)


def kernel(x_prompt, x_sample, cache_kv_w128, cache_kv_w512, cache_kv_w2048, state_conv, norm_attn_g, w_in, conv_w, w_out, norm_mlp_g, w_up, w_down, norm_final_g):
    b, t, d_model = x_prompt.shape
    nb, steps, _ = x_sample.shape
    n_s = nb * steps
    assert w_in.shape[0] == 1, "single trunk layer"
    g1 = norm_attn_g[0][None, :]
    g2 = norm_mlp_g[0][None, :]
    gf = norm_final_g[None, :]
    w_in_b = w_in[0].astype(BF16)
    w_out_b = w_out[0].astype(BF16)
    w_up_b = w_up[0].astype(BF16)
    w_down_b = w_down[0].astype(BF16)
    cw = conv_w[0]

    caches = [jnp.transpose(c[0], (0, 2, 3, 4, 1)).reshape(nb, KV_W, w)
              for c, w in zip((cache_kv_w128, cache_kv_w512, cache_kv_w2048), WINDOWS)]
    state_pad = jnp.pad(state_conv[0], ((0, 0), (0, steps - (CONV_K - 1)), (0, 0))).reshape(n_s, CONV_CH)
    q_s, kn, vn, co_s, u_s = _proj_sample(x_sample.reshape(n_s, d_model), g1, w_in_b, cw, state_pad)
    r3 = lambda a: a.reshape(nb, steps, ATTN_W)
    qkv0, qkv1, qkv2, co, kv0, kv1, kv2, ctail, o_s, n0, n1, n2 = _proj_prompt(
        x_prompt, g1, w_in_b, cw, r3(q_s), r3(kn), r3(vn), caches)
    y_s = _post(x_sample.reshape(1, n_s, d_model), [o_s.reshape(1, n_s, ATTN_W)], co_s.reshape(1, n_s, CONV_CH),
                w_out_b, g2, w_up_b, w_down_b, gf, False)

    attn = [_attn_prompt(qkv, g) for g, qkv in enumerate((qkv0, qkv1, qkv2))]
    attn_inputs = [o for o, _ in attn] + [l for _, l in attn]
    y_p = _post(x_prompt, attn_inputs, co, w_out_b, g2, w_up_b, w_down_b, gf, True)

    kv_p = lambda a, w: a.reshape(1, b, w, 2, HEADS, HEAD_DIM)
    kv_s = lambda a, w: jnp.transpose(a.reshape(nb, 2, HEADS, HEAD_DIM, w), (0, 4, 1, 2, 3))[None]
    return (y_p,
            y_s.reshape(nb, steps, d_model),
            kv_p(kv0, WINDOWS[0]),
            kv_p(kv1, WINDOWS[1]),
            kv_p(kv2, WINDOWS[2]),
            ctail[:, SUBLANES - (CONV_K - 1):, :][None],
            kv_s(n0, WINDOWS[0]),
            kv_s(n1, WINDOWS[1]),
            kv_s(n2, WINDOWS[2]),
            u_s.reshape(nb, steps, CONV_CH)[:, steps - (CONV_K - 1):, :][None])
```

```python
import functools

import jax
import jax.numpy as jnp
from jax import lax
from jax.experimental import pallas as pl
from jax.experimental.pallas import tpu as pltpu

F32 = jnp.float32
BF16 = jnp.bfloat16

EPS = 1e-6
N_GROUPS = 3
HEADS = 4
HEAD_DIM = 64
GROUP_W = HEADS * HEAD_DIM
ATTN_W = N_GROUPS * GROUP_W
QKV_W = 3 * GROUP_W
CONV_CH = 256
CONV_K = 3
KV_W = 2 * GROUP_W
WINDOWS = (128, 512, 2048)
DILATIONS = (1, 4, 16)
SPAN = 128
QK_SCALE = HEAD_DIM ** -0.5
LOG2E = 1.4426950408889634
LN2 = 0.6931471805599453

_Q0, _K0, _V0, _B0, _C0, _H0, _END = 0, 768, 1536, 2304, 2560, 2816, 3072

SUBLANES = 8
LANES = 128
VMEM_LIMIT = 56 * 1024 * 1024

PROJ_TM = 512
POST_TM = 512
ATTN_SB = 4096
TILE = 128
NT_DIMS = (((1,), (1,)), ((), ()))


def _rmsnorm(x, g):
    y = x * lax.rsqrt(jnp.mean(x * x, axis=-1, keepdims=True) + EPS)
    return y * g


def _head_mask(h, dtype):
    lane = lax.broadcasted_iota(jnp.int32, (1, GROUP_W), 1)
    return ((lane >= h * HEAD_DIM) & (lane < (h + 1) * HEAD_DIM)).astype(F32).astype(dtype)


def _in_head(lane, h):
    return (lane >= h * HEAD_DIM) & (lane < (h + 1) * HEAD_DIM)


def _gated_conv(gate_b, u, prev1, prev2, cw_ref):
    y = cw_ref[0:1, :] * prev2
    y = y + cw_ref[1:2, :] * prev1
    y = y + cw_ref[2:3, :] * u
    return gate_b * y


def _proj_prompt_kernel(x_ref, g_ref, w_ref, cw_ref, qs_ref, kn_ref, vn_ref, c0_ref, c1_ref, c2_ref,
                        qkv0_ref, qkv1_ref, qkv2_ref, co_ref, kv0_ref, kv1_ref, kv2_ref, ct_ref,
                        os_ref, n0_ref, n1_ref, n2_ref,
                        carry_ref, slab_ref, *, tm, steps):
    i = pl.program_id(1)

    @pl.when(i == 0)
    def _():
        carry_ref[...] = jnp.zeros_like(carry_ref)

    c_refs = (c0_ref, c1_ref, c2_ref)
    sample = [_sample_scores(bb, qs_ref, kn_ref, vn_ref, c_refs, (n0_ref, n1_ref, n2_ref), steps)
              for bb in range(qs_ref.shape[0])]

    xn = _rmsnorm(x_ref[0], g_ref[...]).astype(BF16)

    def proj(lo, hi):
        return jnp.dot(xn, w_ref[:, lo:hi], preferred_element_type=F32)

    qf = proj(_Q0, _K0) * (QK_SCALE * LOG2E)
    kf = proj(_K0, _V0)
    vf = proj(_V0, _B0)
    for g, (qkv_ref, kv_ref) in enumerate(((qkv0_ref, kv0_ref), (qkv1_ref, kv1_ref), (qkv2_ref, kv2_ref))):
        cols = slice(g * GROUP_W, (g + 1) * GROUP_W)
        rows = kv_ref.shape[1]
        kv_ref[0, :, 0:GROUP_W] = kf[tm - rows:tm, cols]
        kv_ref[0, :, GROUP_W:KV_W] = vf[tm - rows:tm, cols]
        d = DILATIONS[g]
        piece = jnp.concatenate([qf[:, cols], kf[:, cols], vf[:, cols]], axis=1)
        if d == 1:
            qkv_ref[0, 0] = piece.astype(BF16)
        else:
            slabs = slab_ref.at[g - 1]
            for s in range(QKV_W // LANES):
                slabs[s] = piece[:, s * LANES:(s + 1) * LANES]
            for r in range(d):
                for s in range(QKV_W // LANES):
                    qkv_ref[0, r, :, s * LANES:(s + 1) * LANES] = (
                        slabs[s, pl.ds(r, tm // d, stride=d), :].astype(BF16))

    for bb, parts in enumerate(sample):
        _sample_values(bb, parts, c_refs, os_ref, steps)

    gate_b = proj(_B0, _C0)
    u = proj(_C0, _H0) * proj(_H0, _END)
    row = lax.broadcasted_iota(jnp.int32, (tm, CONV_CH), 0)
    carry = carry_ref[...]
    prev1 = jnp.where(row == 0, carry[7:8, :], pltpu.roll(u, 1, 0))
    prev2 = jnp.where(row == 0, carry[6:7, :],
                      jnp.where(row == 1, carry[7:8, :], pltpu.roll(u, 2, 0)))
    co_ref[0] = _gated_conv(gate_b, u, prev1, prev2, cw_ref).astype(BF16)
    tail = u[tm - SUBLANES:tm, :]
    carry_ref[...] = tail
    ct_ref[0] = tail


def _proj_prompt(x, g, w_in, conv_w, q_s, kn_s, vn_s, caches):
    b, t, d_model = x.shape
    tm = PROJ_TM
    nt = t // tm
    assert t % tm == 0 and t >= WINDOWS[-1]
    nb, steps, _ = q_s.shape
    bps = nb // (b * nt)
    assert bps * b * nt == nb and steps == SUBLANES
    assert all(c.shape[1:] == (KV_W, w) for c, w in zip(caches, WINDOWS))
    stok = pl.BlockSpec((bps, steps, ATTN_W), lambda bi, i: (bi * nt + i, 0, 0))
    cspecs = [pl.BlockSpec((bps, KV_W, w), lambda bi, i: (bi * nt + i, 0, 0)) for w in WINDOWS]

    def tail_spec(w):
        rows = min(w, tm)
        nblk = max(w // tm, 1)
        return (pl.BlockSpec((1, rows, KV_W), lambda bi, i: (bi, jnp.maximum(i - (nt - nblk), 0), 0)),
                jax.ShapeDtypeStruct((b, w, KV_W), F32))

    tails = [tail_spec(w) for w in WINDOWS]
    tok = lambda width: pl.BlockSpec((1, tm, width), lambda bi, i: (bi, i, 0))
    const = lambda shape: pl.BlockSpec(shape, lambda bi, i: (0, 0))
    qkv_specs = [pl.BlockSpec((1, d, tm // d, QKV_W), lambda bi, i: (bi, 0, i, 0)) for d in DILATIONS]
    qkv_shapes = [jax.ShapeDtypeStruct((b, d, t // d, QKV_W), BF16) for d in DILATIONS]
    return pl.pallas_call(
        functools.partial(_proj_prompt_kernel, tm=tm, steps=steps),
        grid=(b, nt),
        in_specs=[tok(d_model), const((1, d_model)), const((d_model, _END)), const((CONV_K, CONV_CH)),
                  stok, stok, stok] + cspecs,
        out_specs=qkv_specs + [tok(CONV_CH)] + [s for s, _ in tails]
                  + [pl.BlockSpec((1, SUBLANES, CONV_CH), lambda bi, i: (bi, 0, 0)), stok] + cspecs,
        out_shape=qkv_shapes + [jax.ShapeDtypeStruct((b, t, CONV_CH), BF16)] + [s for _, s in tails]
                  + [jax.ShapeDtypeStruct((b, SUBLANES, CONV_CH), F32),
                     jax.ShapeDtypeStruct((nb, steps, ATTN_W), F32)]
                  + [jax.ShapeDtypeStruct((nb, KV_W, w), F32) for w in WINDOWS],
        scratch_shapes=[pltpu.VMEM((SUBLANES, CONV_CH), F32),
                        pltpu.VMEM((N_GROUPS - 1, QKV_W // LANES, tm, LANES), F32)],
        compiler_params=pltpu.CompilerParams(
            dimension_semantics=("arbitrary", "arbitrary"), vmem_limit_bytes=VMEM_LIMIT),
        name="proj_prompt",
    )(x, g, w_in, conv_w, q_s, kn_s, vn_s, *caches)


def _proj_sample_kernel(x_ref, g_ref, w_ref, cw_ref, sp_ref,
                        q_ref, kn_ref, vn_ref, co_ref, u_ref, *, tm):
    xn = _rmsnorm(x_ref[...], g_ref[...]).astype(BF16)

    def proj(lo, hi):
        return jnp.dot(xn, w_ref[:, lo:hi], preferred_element_type=F32)

    q_ref[...] = proj(_Q0, _K0) * QK_SCALE
    kn_ref[...] = proj(_K0, _V0)
    vn_ref[...] = proj(_V0, _B0)
    gate_b = proj(_B0, _C0)
    u = proj(_C0, _H0) * proj(_H0, _END)
    step = lax.broadcasted_iota(jnp.int32, (tm, CONV_CH), 0) & (SUBLANES - 1)
    sp = sp_ref[...]
    prev1 = jnp.where(step == 0, pltpu.roll(sp, tm - 1, 0), pltpu.roll(u, 1, 0))
    prev2 = jnp.where(step < 2, sp, pltpu.roll(u, 2, 0))
    co_ref[...] = _gated_conv(gate_b, u, prev1, prev2, cw_ref).astype(BF16)
    u_ref[...] = u


def _proj_sample(x, g, w_in, conv_w, state_pad):
    n, d_model = x.shape
    tm = min(n, POST_TM)
    assert n % tm == 0 and tm % SUBLANES == 0
    tok = lambda width: pl.BlockSpec((tm, width), lambda i: (i, 0))
    const = lambda shape: pl.BlockSpec(shape, lambda i: (0, 0))
    return pl.pallas_call(
        functools.partial(_proj_sample_kernel, tm=tm),
        grid=(n // tm,),
        in_specs=[tok(d_model), const((1, d_model)), const((d_model, _END)), const((CONV_K, CONV_CH)),
                  tok(CONV_CH)],
        out_specs=[tok(ATTN_W), tok(ATTN_W), tok(ATTN_W), tok(CONV_CH), tok(CONV_CH)],
        out_shape=[jax.ShapeDtypeStruct((n, ATTN_W), F32)] * 3
                  + [jax.ShapeDtypeStruct((n, CONV_CH), BF16), jax.ShapeDtypeStruct((n, CONV_CH), F32)],
        compiler_params=pltpu.CompilerParams(
            dimension_semantics=("arbitrary",), vmem_limit_bytes=VMEM_LIMIT),
        name="proj_sample",
    )(x, g, w_in, conv_w, state_pad)


def _attn_prompt_kernel(qkv_ref, kp_ref, vp_ref, o_ref, l_ref, *, d, n_tiles):
    sb = pl.program_id(1)

    key = lax.broadcasted_iota(jnp.int32, (TILE + SPAN, TILE), 0)
    qry = lax.broadcasted_iota(jnp.int32, (TILE + SPAN, TILE), 1)
    band = (key >= qry) & (key <= qry + SPAN)
    band_first = band & (key + jnp.where(sb == 0, 0, SPAN) >= SPAN)
    k_cols = slice(GROUP_W, 2 * GROUP_W)
    v_cols = slice(2 * GROUP_W, 3 * GROUP_W)

    def keys_values(r, j, cols, prev_ref):
        if j == 0:
            return jnp.concatenate([prev_ref[0, r], qkv_ref[0, r, 0:TILE, cols]], axis=0)
        return qkv_ref[0, r, (j - 1) * TILE:(j + 1) * TILE, cols]

    def scores(r, j):
        q = qkv_ref[0, r, j * TILE:(j + 1) * TILE, 0:GROUP_W]
        kk = keys_values(r, j, k_cols, kp_ref)
        ss = []
        for h in range(0, HEADS, 2):
            q2 = jnp.concatenate([q * _head_mask(h, BF16), q * _head_mask(h + 1, BF16)], axis=0)
            s2 = lax.dot_general(kk, q2, NT_DIMS, preferred_element_type=F32)
            ss += [s2[:, 0:TILE], s2[:, TILE:2 * TILE]]
        return ss

    def finish(r, j, ss):
        vt = keys_values(r, j, v_cols, vp_ref).T
        valid = band_first if j == 0 else band
        o_t, l_t = [], []
        for h in range(HEADS):
            s = jnp.where(valid, ss[h], -jnp.inf)
            m = jnp.max(s, axis=0, keepdims=True)
            p = jnp.exp2(s - m)
            z = jnp.sum(p, axis=0, keepdims=True)
            acc = jnp.dot(vt[h * HEAD_DIM:(h + 1) * HEAD_DIM, :], p.astype(BF16),
                          preferred_element_type=F32)
            o_t.append(acc * (1.0 / z))
            l_t.append(jnp.broadcast_to(m * LN2 + jnp.log(z), (HEAD_DIM, TILE)))
        o_acc = jnp.concatenate(o_t, axis=0).T
        l_acc = jnp.concatenate(l_t, axis=0).T
        if d == 1:
            rows = pl.ds(j * TILE, TILE)
        else:
            rows = pl.ds(j * TILE * d + r, TILE, stride=d)
        for half in range(GROUP_W // LANES):
            lanes = slice(half * LANES, (half + 1) * LANES)
            o_ref[0, half, rows, :] = o_acc[:, lanes]
            l_ref[0, half, rows, :] = l_acc[:, lanes]

    units = [(r, j) for r in range(d) for j in range(n_tiles)]
    ss_next = scores(*units[0])
    for n, unit in enumerate(units):
        ss = ss_next
        if n + 1 < len(units):
            ss_next = scores(*units[n + 1])
        finish(*unit, ss)


def _attn_prompt(qkv, g):
    b, d, tc, _ = qkv.shape
    t = tc * d
    sb = min(ATTN_SB, t)
    n_tiles = sb // d // TILE
    assert d == DILATIONS[g] and t % sb == 0 and sb % (d * TILE) == 0 and TILE == SPAN

    def prev_spec(col_block):
        return pl.BlockSpec((1, d, SPAN, GROUP_W),
                            lambda bi, s: (bi, 0, jnp.maximum(s * n_tiles - 1, 0), col_block))

    return pl.pallas_call(
        functools.partial(_attn_prompt_kernel, d=d, n_tiles=n_tiles),
        grid=(b, t // sb),
        in_specs=[pl.BlockSpec((1, d, sb // d, QKV_W), lambda bi, s: (bi, 0, s, 0)),
                  prev_spec(1), prev_spec(2)],
        out_specs=[pl.BlockSpec((1, 2, sb, LANES), lambda bi, s: (bi, 0, s, 0))] * 2,
        out_shape=[jax.ShapeDtypeStruct((b, 2, t, LANES), F32)] * 2,
        compiler_params=pltpu.CompilerParams(
            dimension_semantics=("arbitrary", "arbitrary"), vmem_limit_bytes=VMEM_LIMIT),
        name=f"attn_prompt_g{g}",
    )(qkv, qkv, qkv)


def _sample_scores(bb, q_ref, kn_ref, vn_ref, c_refs, n_refs, steps):
    rows = HEADS * steps
    step_of = lambda shape: lax.broadcasted_iota(jnp.int32, shape, 0) & (steps - 1)
    lane_kv = lax.broadcasted_iota(jnp.int32, (KV_W, LANES), 1)
    pad = jnp.zeros((LANES - steps, GROUP_W), F32)
    parts = []
    for g, (c_ref, n_ref) in enumerate(zip(c_refs, n_refs)):
        w = c_ref.shape[2]
        d = DILATIONS[g]
        kn = kn_ref[bb, :, g * GROUP_W:(g + 1) * GROUP_W]
        vn = vn_ref[bb, :, g * GROUP_W:(g + 1) * GROUP_W]
        knt = jnp.concatenate([kn, pad], axis=0).T
        vnt = jnp.concatenate([vn, pad], axis=0).T
        shift = LANES - steps
        nxt = pltpu.roll(c_ref[bb, :, 0:LANES], shift, 1)
        for cb in range(w // LANES):
            cur = nxt
            if cb + 1 < w // LANES:
                nxt = pltpu.roll(c_ref[bb, :, (cb + 1) * LANES:(cb + 2) * LANES], shift, 1)
            else:
                nxt = pltpu.roll(jnp.concatenate([knt, vnt], axis=0), shift, 1)
            n_ref[bb, :, cb * LANES:(cb + 1) * LANES] = jnp.where(lane_kv < shift, cur, nxt)

        qg = q_ref[bb, :, g * GROUP_W:(g + 1) * GROUP_W]
        qs = jnp.concatenate([qg * _head_mask(h, F32) for h in range(HEADS)], axis=0).astype(BF16)
        ck = c_ref[bb, 0:GROUP_W, :].astype(BF16)
        s_c = jnp.dot(qs, ck, preferred_element_type=F32)
        s_n = jnp.dot(qs, knt.astype(BF16), preferred_element_type=F32)
        t_c = step_of((rows, w))
        i_c = lax.broadcasted_iota(jnp.int32, (rows, w), 1)
        valid_c = (i_c >= t_c) & (((i_c - t_c) & (d - 1)) == 0)
        t_n = step_of((rows, LANES))
        i_n = lax.broadcasted_iota(jnp.int32, (rows, LANES), 1)
        valid_n = (i_n <= t_n) & (((t_n - i_n) & (d - 1)) == 0)
        s_c = jnp.where(valid_c, s_c, -jnp.inf)
        s_n = jnp.where(valid_n, s_n, -jnp.inf)
        m = jnp.maximum(jnp.max(s_c, axis=-1, keepdims=True), jnp.max(s_n, axis=-1, keepdims=True))
        p_c = jnp.exp(s_c - m)
        p_n = jnp.exp(s_n - m)
        z = jnp.sum(p_c, axis=-1, keepdims=True) + jnp.sum(p_n, axis=-1, keepdims=True)
        parts.append((p_c.astype(BF16), p_n.astype(BF16), z, m, vnt.astype(BF16)))
    return parts


def _sample_values(bb, parts, c_refs, o_ref, steps):
    lane256 = lax.broadcasted_iota(jnp.int32, (steps, GROUP_W), 1)
    outs, lses = [], []
    for (p_c, p_n, z, m, vnt), c_ref in zip(parts, c_refs):
        cv = c_ref[bb, GROUP_W:KV_W, :].astype(BF16)
        of = lax.dot_general(p_c, cv, NT_DIMS, preferred_element_type=F32)
        of = of + lax.dot_general(p_n, vnt, NT_DIMS, preferred_element_type=F32)
        outs.append(of / z)
        lses.append(m + jnp.log(z))
    top = jnp.maximum(jnp.maximum(lses[0], lses[1]), lses[2])
    es = [jnp.exp(l - top) for l in lses]
    den = es[0] + es[1] + es[2]
    for g in range(N_GROUPS):
        og = outs[g] * (es[g] / den)
        res = jnp.zeros((steps, GROUP_W), F32)
        for h in range(HEADS):
            res = jnp.where(_in_head(lane256, h), og[h * steps:(h + 1) * steps, :], res)
        o_ref[bb, :, g * GROUP_W:(g + 1) * GROUP_W] = res


def _post_kernel(*refs, tm, combine):
    if combine:
        (x_ref, o0_ref, o1_ref, o2_ref, l0_ref, l1_ref, l2_ref, co_ref,
         wo_ref, g2_ref, wu_ref, wd_ref, gf_ref, y_ref) = refs
    else:
        x_ref, o_ref, co_ref, wo_ref, g2_ref, wu_ref, wd_ref, gf_ref, y_ref = refs
    halves = [pl.ds(k * (tm // 2), tm // 2) for k in range(2)]

    def mixer_input(rows):
        if not combine:
            return jnp.concatenate([o_ref[0, rows, :].astype(BF16), co_ref[0, rows, :]], axis=-1)
        slabs = lambda ref: jnp.concatenate([ref[0, 0, rows, :], ref[0, 1, rows, :]], axis=1)
        ls = [slabs(l_ref) for l_ref in (l0_ref, l1_ref, l2_ref)]
        top = jnp.maximum(jnp.maximum(ls[0], ls[1]), ls[2])
        es = [jnp.exp(l - top) for l in ls]
        den = es[0] + es[1] + es[2]
        parts = [(slabs(o_ref) * (e / den)).astype(BF16) for o_ref, e in zip((o0_ref, o1_ref, o2_ref), es)]
        return jnp.concatenate(parts + [co_ref[0, rows, :]], axis=-1)

    chunk = 1024
    n_chunks = wu_ref.shape[1] // chunk

    def mlp(xn):
        def up(c):
            hid = jnp.dot(xn, wu_ref[:, c * chunk:(c + 1) * chunk], preferred_element_type=F32)
            return jnp.square(jnp.maximum(hid, 0.0)).astype(BF16)

        acc = None
        hid_next = up(0)
        for c in range(n_chunks):
            hid = hid_next
            if c + 1 < n_chunks:
                hid_next = up(c + 1)
            part = jnp.dot(hid, wd_ref[c * chunk:(c + 1) * chunk, :], preferred_element_type=F32)
            acc = part if acc is None else acc + part
        return acc

    mixed = [mixer_input(rows) for rows in halves]
    x1 = [x_ref[0, rows, :] + jnp.dot(m, wo_ref[...], preferred_element_type=F32)
          for rows, m in zip(halves, mixed)]
    xn = [_rmsnorm(v, g2_ref[...]).astype(BF16) for v in x1]
    for rows, v, n in zip(halves, x1, xn):
        y_ref[0, rows, :] = _rmsnorm(v + mlp(n), gf_ref[...])


def _post(x, attn_inputs, co, w_out, g2, w_up, w_down, gf, combine):
    b, t, d_model = x.shape
    tm = min(t, POST_TM)
    assert t % tm == 0

    def tok(a):
        if a.ndim == 4:
            return pl.BlockSpec((1, a.shape[1], tm, a.shape[3]), lambda bi, i: (bi, 0, i, 0))
        return pl.BlockSpec((1, tm, a.shape[2]), lambda bi, i: (bi, i, 0))

    const = lambda a: pl.BlockSpec(a.shape, lambda bi, i: (0, 0))
    return pl.pallas_call(
        functools.partial(_post_kernel, tm=tm, combine=combine),
        grid=(b, t // tm),
        in_specs=[tok(x)] + [tok(a) for a in attn_inputs]
                 + [tok(co), const(w_out), const(g2), const(w_up), const(w_down), const(gf)],
        out_specs=tok(x),
        out_shape=jax.ShapeDtypeStruct((b, t, d_model), F32),
        compiler_params=pltpu.CompilerParams(
            dimension_semantics=("arbitrary", "arbitrary"), vmem_limit_bytes=VMEM_LIMIT),
        name="post_prompt" if combine else "post_sample",
    )(x, *attn_inputs, co, w_out, g2, w_up, w_down, gf)


def kernel(x_prompt, x_sample, cache_kv_w128, cache_kv_w512, cache_kv_w2048, state_conv, norm_attn_g, w_in, conv_w, w_out, norm_mlp_g, w_up, w_down, norm_final_g):
    b, t, d_model = x_prompt.shape
    nb, steps, _ = x_sample.shape
    n_s = nb * steps
    assert w_in.shape[0] == 1, "single trunk layer"
    g1 = norm_attn_g[0][None, :]
    g2 = norm_mlp_g[0][None, :]
    gf = norm_final_g[None, :]
    w_in_b = w_in[0].astype(BF16)
    w_out_b = w_out[0].astype(BF16)
    w_up_b = w_up[0].astype(BF16)
    w_down_b = w_down[0].astype(BF16)
    cw = conv_w[0]

    caches = [jnp.transpose(c[0], (0, 2, 3, 4, 1)).reshape(nb, KV_W, w)
              for c, w in zip((cache_kv_w128, cache_kv_w512, cache_kv_w2048), WINDOWS)]
    state_pad = jnp.pad(state_conv[0], ((0, 0), (0, steps - (CONV_K - 1)), (0, 0))).reshape(n_s, CONV_CH)
    q_s, kn, vn, co_s, u_s = _proj_sample(x_sample.reshape(n_s, d_model), g1, w_in_b, cw, state_pad)
    r3 = lambda a: a.reshape(nb, steps, ATTN_W)
    qkv0, qkv1, qkv2, co, kv0, kv1, kv2, ctail, o_s, n0, n1, n2 = _proj_prompt(
        x_prompt, g1, w_in_b, cw, r3(q_s), r3(kn), r3(vn), caches)
    y_s = _post(x_sample.reshape(1, n_s, d_model), [o_s.reshape(1, n_s, ATTN_W)], co_s.reshape(1, n_s, CONV_CH),
                w_out_b, g2, w_up_b, w_down_b, gf, False)

    attn = [_attn_prompt(qkv, g) for g, qkv in enumerate((qkv0, qkv1, qkv2))]
    attn_inputs = [o for o, _ in attn] + [l for _, l in attn]
    y_p = _post(x_prompt, attn_inputs, co, w_out_b, g2, w_up_b, w_down_b, gf, True)

    kv_p = lambda a, w: a.reshape(1, b, w, 2, HEADS, HEAD_DIM)
    kv_s = lambda a, w: jnp.transpose(a.reshape(nb, 2, HEADS, HEAD_DIM, w), (0, 4, 1, 2, 3))[None]
    return (y_p,
            y_s.reshape(nb, steps, d_model),
            kv_p(kv0, WINDOWS[0]),
            kv_p(kv1, WINDOWS[1]),
            kv_p(kv2, WINDOWS[2]),
            ctail[:, SUBLANES - (CONV_K - 1):, :][None],
            kv_s(n0, WINDOWS[0]),
            kv_s(n1, WINDOWS[1]),
            kv_s(n2, WINDOWS[2]),
            u_s.reshape(nb, steps, CONV_CH)[:, steps - (CONV_K - 1):, :][None])
```

```python
import functools

import jax
import jax.numpy as jnp
from jax import lax
from jax.experimental import pallas as pl
from jax.experimental.pallas import tpu as pltpu

F32 = jnp.float32
BF16 = jnp.bfloat16

EPS = 1e-6
N_GROUPS = 3
HEADS = 4
HEAD_DIM = 64
GROUP_W = HEADS * HEAD_DIM
ATTN_W = N_GROUPS * GROUP_W
QKV_W = 3 * GROUP_W
CONV_CH = 256
CONV_K = 3
KV_W = 2 * GROUP_W
WINDOWS = (128, 512, 2048)
DILATIONS = (1, 4, 16)
SPAN = 128
QK_SCALE = HEAD_DIM ** -0.5
LOG2E = 1.4426950408889634
LN2 = 0.6931471805599453

_Q0, _K0, _V0, _B0, _C0, _H0, _END = 0, 768, 1536, 2304, 2560, 2816, 3072

SUBLANES = 8
LANES = 128
VMEM_LIMIT = 56 * 1024 * 1024

PROJ_TM = 512
POST_TM = 512
ATTN_SB = 4096
TILE = 128
NT_DIMS = (((1,), (1,)), ((), ()))


def _rmsnorm(x, g):
    y = x * lax.rsqrt(jnp.mean(x * x, axis=-1, keepdims=True) + EPS)
    return y * g


def _head_mask(h, dtype):
    lane = lax.broadcasted_iota(jnp.int32, (1, GROUP_W), 1)
    return ((lane >= h * HEAD_DIM) & (lane < (h + 1) * HEAD_DIM)).astype(F32).astype(dtype)


def _in_head(lane, h):
    return (lane >= h * HEAD_DIM) & (lane < (h + 1) * HEAD_DIM)


def _gated_conv(gate_b, u, prev1, prev2, cw_ref):
    y = cw_ref[0:1, :] * prev2
    y = y + cw_ref[1:2, :] * prev1
    y = y + cw_ref[2:3, :] * u
    return gate_b * y


def _proj_prompt_kernel(x_ref, g_ref, w_ref, cw_ref, qs_ref, kn_ref, vn_ref, c0_ref, c1_ref, c2_ref,
                        qkv0_ref, qkv1_ref, qkv2_ref, co_ref, kv0_ref, kv1_ref, kv2_ref, ct_ref,
                        os_ref, n0_ref, n1_ref,
                        carry_ref, slab_ref, *, tm, steps):
    i = pl.program_id(1)

    @pl.when(i == 0)
    def _():
        carry_ref[...] = jnp.zeros_like(carry_ref)

    c_refs = (c0_ref, c1_ref, c2_ref)
    sample = [_sample_scores(bb, qs_ref, kn_ref, vn_ref, c_refs, (n0_ref, n1_ref), steps)
              for bb in range(qs_ref.shape[0])]

    xn = _rmsnorm(x_ref[0], g_ref[...]).astype(BF16)

    def proj(lo, hi):
        return jnp.dot(xn, w_ref[:, lo:hi], preferred_element_type=F32)

    qf = proj(_Q0, _K0) * (QK_SCALE * LOG2E)
    kf = proj(_K0, _V0)
    vf = proj(_V0, _B0)
    for g, (qkv_ref, kv_ref) in enumerate(((qkv0_ref, kv0_ref), (qkv1_ref, kv1_ref), (qkv2_ref, kv2_ref))):
        cols = slice(g * GROUP_W, (g + 1) * GROUP_W)
        rows = kv_ref.shape[1]
        kv_ref[0, :, 0:GROUP_W] = kf[tm - rows:tm, cols]
        kv_ref[0, :, GROUP_W:KV_W] = vf[tm - rows:tm, cols]
        d = DILATIONS[g]
        piece = jnp.concatenate([qf[:, cols], kf[:, cols], vf[:, cols]], axis=1)
        if d == 1:
            qkv_ref[0, 0] = piece.astype(BF16)
        else:
            slabs = slab_ref.at[g - 1]
            for s in range(QKV_W // LANES):
                slabs[s] = piece[:, s * LANES:(s + 1) * LANES]
            for r in range(d):
                for s in range(QKV_W // LANES):
                    qkv_ref[0, r, :, s * LANES:(s + 1) * LANES] = (
                        slabs[s, pl.ds(r, tm // d, stride=d), :].astype(BF16))

    for bb, parts in enumerate(sample):
        _sample_values(bb, parts, c_refs, os_ref, steps)

    gate_b = proj(_B0, _C0)
    u = proj(_C0, _H0) * proj(_H0, _END)
    row = lax.broadcasted_iota(jnp.int32, (tm, CONV_CH), 0)
    carry = carry_ref[...]
    prev1 = jnp.where(row == 0, carry[7:8, :], pltpu.roll(u, 1, 0))
    prev2 = jnp.where(row == 0, carry[6:7, :],
                      jnp.where(row == 1, carry[7:8, :], pltpu.roll(u, 2, 0)))
    co_ref[0] = _gated_conv(gate_b, u, prev1, prev2, cw_ref).astype(BF16)
    tail = u[tm - SUBLANES:tm, :]
    carry_ref[...] = tail
    ct_ref[0] = tail


def _proj_prompt(x, g, w_in, conv_w, q_s, kn_s, vn_s, caches):
    b, t, d_model = x.shape
    tm = PROJ_TM
    nt = t // tm
    assert t % tm == 0 and t >= WINDOWS[-1]
    nb, steps, _ = q_s.shape
    bps = nb // (b * nt)
    assert bps * b * nt == nb and steps == SUBLANES
    assert all(c.shape[1:] == (KV_W, w) for c, w in zip(caches, WINDOWS))
    stok = pl.BlockSpec((bps, steps, ATTN_W), lambda bi, i: (bi * nt + i, 0, 0))
    cspecs = [pl.BlockSpec((bps, KV_W, w), lambda bi, i: (bi * nt + i, 0, 0)) for w in WINDOWS]

    def tail_spec(w):
        rows = min(w, tm)
        nblk = max(w // tm, 1)
        return (pl.BlockSpec((1, rows, KV_W), lambda bi, i: (bi, jnp.maximum(i - (nt - nblk), 0), 0)),
                jax.ShapeDtypeStruct((b, w, KV_W), F32))

    tails = [tail_spec(w) for w in WINDOWS]
    tok = lambda width: pl.BlockSpec((1, tm, width), lambda bi, i: (bi, i, 0))
    const = lambda shape: pl.BlockSpec(shape, lambda bi, i: (0, 0))
    qkv_specs = [pl.BlockSpec((1, d, tm // d, QKV_W), lambda bi, i: (bi, 0, i, 0)) for d in DILATIONS]
    qkv_shapes = [jax.ShapeDtypeStruct((b, d, t // d, QKV_W), BF16) for d in DILATIONS]
    return pl.pallas_call(
        functools.partial(_proj_prompt_kernel, tm=tm, steps=steps),
        grid=(b, nt),
        in_specs=[tok(d_model), const((1, d_model)), const((d_model, _END)), const((CONV_K, CONV_CH)),
                  stok, stok, stok] + cspecs,
        out_specs=qkv_specs + [tok(CONV_CH)] + [s for s, _ in tails]
                  + [pl.BlockSpec((1, SUBLANES, CONV_CH), lambda bi, i: (bi, 0, 0)), stok] + cspecs[:2],
        out_shape=qkv_shapes + [jax.ShapeDtypeStruct((b, t, CONV_CH), BF16)] + [s for _, s in tails]
                  + [jax.ShapeDtypeStruct((b, SUBLANES, CONV_CH), F32),
                     jax.ShapeDtypeStruct((nb, steps, ATTN_W), F32)]
                  + [jax.ShapeDtypeStruct((nb, KV_W, w), F32) for w in WINDOWS[:2]],
        scratch_shapes=[pltpu.VMEM((SUBLANES, CONV_CH), F32),
                        pltpu.VMEM((N_GROUPS - 1, QKV_W // LANES, tm, LANES), F32)],
        compiler_params=pltpu.CompilerParams(
            dimension_semantics=("arbitrary", "arbitrary"), vmem_limit_bytes=VMEM_LIMIT),
        name="proj_prompt",
    )(x, g, w_in, conv_w, q_s, kn_s, vn_s, *caches)


def _proj_sample_kernel(x_ref, g_ref, w_ref, cw_ref, sp_ref,
                        q_ref, kn_ref, vn_ref, co_ref, u_ref, *, tm):
    xn = _rmsnorm(x_ref[...], g_ref[...]).astype(BF16)

    def proj(lo, hi):
        return jnp.dot(xn, w_ref[:, lo:hi], preferred_element_type=F32)

    q_ref[...] = proj(_Q0, _K0) * QK_SCALE
    kn_ref[...] = proj(_K0, _V0)
    vn_ref[...] = proj(_V0, _B0)
    gate_b = proj(_B0, _C0)
    u = proj(_C0, _H0) * proj(_H0, _END)
    step = lax.broadcasted_iota(jnp.int32, (tm, CONV_CH), 0) & (SUBLANES - 1)
    sp = sp_ref[...]
    prev1 = jnp.where(step == 0, pltpu.roll(sp, tm - 1, 0), pltpu.roll(u, 1, 0))
    prev2 = jnp.where(step < 2, sp, pltpu.roll(u, 2, 0))
    co_ref[...] = _gated_conv(gate_b, u, prev1, prev2, cw_ref).astype(BF16)
    u_ref[...] = u


def _proj_sample(x, g, w_in, conv_w, state_pad):
    n, d_model = x.shape
    tm = min(n, POST_TM)
    assert n % tm == 0 and tm % SUBLANES == 0
    tok = lambda width: pl.BlockSpec((tm, width), lambda i: (i, 0))
    const = lambda shape: pl.BlockSpec(shape, lambda i: (0, 0))
    return pl.pallas_call(
        functools.partial(_proj_sample_kernel, tm=tm),
        grid=(n // tm,),
        in_specs=[tok(d_model), const((1, d_model)), const((d_model, _END)), const((CONV_K, CONV_CH)),
                  tok(CONV_CH)],
        out_specs=[tok(ATTN_W), tok(ATTN_W), tok(ATTN_W), tok(CONV_CH), tok(CONV_CH)],
        out_shape=[jax.ShapeDtypeStruct((n, ATTN_W), F32)] * 3
                  + [jax.ShapeDtypeStruct((n, CONV_CH), BF16), jax.ShapeDtypeStruct((n, CONV_CH), F32)],
        compiler_params=pltpu.CompilerParams(
            dimension_semantics=("arbitrary",), vmem_limit_bytes=VMEM_LIMIT),
        name="proj_sample",
    )(x, g, w_in, conv_w, state_pad)


def _attn_prompt_kernel(qkv_ref, kp_ref, vp_ref, o_ref, l_ref, *, d, n_tiles):
    sb = pl.program_id(1)

    key = lax.broadcasted_iota(jnp.int32, (TILE + SPAN, TILE), 0)
    qry = lax.broadcasted_iota(jnp.int32, (TILE + SPAN, TILE), 1)
    band = (key >= qry) & (key <= qry + SPAN)
    band_first = band & (key + jnp.where(sb == 0, 0, SPAN) >= SPAN)
    k_cols = slice(GROUP_W, 2 * GROUP_W)
    v_cols = slice(2 * GROUP_W, 3 * GROUP_W)

    def keys_values(r, j, cols, prev_ref):
        if j == 0:
            return jnp.concatenate([prev_ref[0, r], qkv_ref[0, r, 0:TILE, cols]], axis=0)
        return qkv_ref[0, r, (j - 1) * TILE:(j + 1) * TILE, cols]

    def scores(r, j):
        q = qkv_ref[0, r, j * TILE:(j + 1) * TILE, 0:GROUP_W]
        kk = keys_values(r, j, k_cols, kp_ref)
        ss = []
        for h in range(0, HEADS, 2):
            q2 = jnp.concatenate([q * _head_mask(h, BF16), q * _head_mask(h + 1, BF16)], axis=0)
            s2 = lax.dot_general(kk, q2, NT_DIMS, preferred_element_type=F32)
            ss += [s2[:, 0:TILE], s2[:, TILE:2 * TILE]]
        return ss

    def finish(r, j, ss):
        vt = keys_values(r, j, v_cols, vp_ref).T
        valid = band_first if j == 0 else band
        o_t, l_t = [], []
        for h in range(HEADS):
            s = jnp.where(valid, ss[h], -jnp.inf)
            m = jnp.max(s, axis=0, keepdims=True)
            p = jnp.exp2(s - m)
            z = jnp.sum(p, axis=0, keepdims=True)
            acc = jnp.dot(vt[h * HEAD_DIM:(h + 1) * HEAD_DIM, :], p.astype(BF16),
                          preferred_element_type=F32)
            o_t.append(acc * (1.0 / z))
            l_t.append(jnp.broadcast_to(m * LN2 + jnp.log(z), (HEAD_DIM, TILE)))
        o_acc = jnp.concatenate(o_t, axis=0).T
        l_acc = jnp.concatenate(l_t, axis=0).T
        if d == 1:
            rows = pl.ds(j * TILE, TILE)
        else:
            rows = pl.ds(j * TILE * d + r, TILE, stride=d)
        for half in range(GROUP_W // LANES):
            lanes = slice(half * LANES, (half + 1) * LANES)
            o_ref[0, half, rows, :] = o_acc[:, lanes]
            l_ref[0, half, rows, :] = l_acc[:, lanes]

    units = [(r, j) for r in range(d) for j in range(n_tiles)]
    ss_next = scores(*units[0])
    for n, unit in enumerate(units):
        ss = ss_next
        if n + 1 < len(units):
            ss_next = scores(*units[n + 1])
        finish(*unit, ss)


def _attn_prompt(qkv, g):
    b, d, tc, _ = qkv.shape
    t = tc * d
    sb = min(ATTN_SB, t)
    n_tiles = sb // d // TILE
    assert d == DILATIONS[g] and t % sb == 0 and sb % (d * TILE) == 0 and TILE == SPAN

    def prev_spec(col_block):
        return pl.BlockSpec((1, d, SPAN, GROUP_W),
                            lambda bi, s: (bi, 0, jnp.maximum(s * n_tiles - 1, 0), col_block))

    return pl.pallas_call(
        functools.partial(_attn_prompt_kernel, d=d, n_tiles=n_tiles),
        grid=(b, t // sb),
        in_specs=[pl.BlockSpec((1, d, sb // d, QKV_W), lambda bi, s: (bi, 0, s, 0)),
                  prev_spec(1), prev_spec(2)],
        out_specs=[pl.BlockSpec((1, 2, sb, LANES), lambda bi, s: (bi, 0, s, 0))] * 2,
        out_shape=[jax.ShapeDtypeStruct((b, 2, t, LANES), F32)] * 2,
        compiler_params=pltpu.CompilerParams(
            dimension_semantics=("arbitrary", "arbitrary"), vmem_limit_bytes=VMEM_LIMIT),
        name=f"attn_prompt_g{g}",
    )(qkv, qkv, qkv)


def _new_rows_transposed(new_ref, bb, g, steps):
    new = new_ref[bb, :, g * GROUP_W:(g + 1) * GROUP_W]
    return jnp.concatenate([new, jnp.zeros((LANES - steps, GROUP_W), F32)], axis=0).T


def _shift_window(bb, c_ref, n_ref, knt, vnt, steps):
    w = c_ref.shape[2]
    shift = LANES - steps
    lane_kv = lax.broadcasted_iota(jnp.int32, (KV_W, LANES), 1)
    nxt = pltpu.roll(c_ref[bb, :, 0:LANES], shift, 1)
    for cb in range(w // LANES):
        cur = nxt
        if cb + 1 < w // LANES:
            nxt = pltpu.roll(c_ref[bb, :, (cb + 1) * LANES:(cb + 2) * LANES], shift, 1)
        else:
            nxt = pltpu.roll(jnp.concatenate([knt, vnt], axis=0), shift, 1)
        n_ref[bb, :, cb * LANES:(cb + 1) * LANES] = jnp.where(lane_kv < shift, cur, nxt)


def _sample_scores(bb, q_ref, kn_ref, vn_ref, c_refs, n_refs, steps):
    rows = HEADS * steps
    step_of = lambda shape: lax.broadcasted_iota(jnp.int32, shape, 0) & (steps - 1)
    parts = []
    for g, c_ref in enumerate(c_refs):
        w = c_ref.shape[2]
        d = DILATIONS[g]
        knt = _new_rows_transposed(kn_ref, bb, g, steps)
        vnt = _new_rows_transposed(vn_ref, bb, g, steps)
        if g < len(n_refs):
            _shift_window(bb, c_ref, n_refs[g], knt, vnt, steps)

        qg = q_ref[bb, :, g * GROUP_W:(g + 1) * GROUP_W]
        qs = jnp.concatenate([qg * _head_mask(h, F32) for h in range(HEADS)], axis=0).astype(BF16)
        ck = c_ref[bb, 0:GROUP_W, :].astype(BF16)
        s_c = jnp.dot(qs, ck, preferred_element_type=F32)
        s_n = jnp.dot(qs, knt.astype(BF16), preferred_element_type=F32)
        t_c = step_of((rows, w))
        i_c = lax.broadcasted_iota(jnp.int32, (rows, w), 1)
        valid_c = (i_c >= t_c) & (((i_c - t_c) & (d - 1)) == 0)
        t_n = step_of((rows, LANES))
        i_n = lax.broadcasted_iota(jnp.int32, (rows, LANES), 1)
        valid_n = (i_n <= t_n) & (((t_n - i_n) & (d - 1)) == 0)
        s_c = jnp.where(valid_c, s_c, -jnp.inf)
        s_n = jnp.where(valid_n, s_n, -jnp.inf)
        m = jnp.maximum(jnp.max(s_c, axis=-1, keepdims=True), jnp.max(s_n, axis=-1, keepdims=True))
        p_c = jnp.exp(s_c - m)
        p_n = jnp.exp(s_n - m)
        z = jnp.sum(p_c, axis=-1, keepdims=True) + jnp.sum(p_n, axis=-1, keepdims=True)
        parts.append((p_c.astype(BF16), p_n.astype(BF16), z, m, vnt.astype(BF16)))
    return parts


def _sample_values(bb, parts, c_refs, o_ref, steps):
    lane256 = lax.broadcasted_iota(jnp.int32, (steps, GROUP_W), 1)
    outs, lses = [], []
    for (p_c, p_n, z, m, vnt), c_ref in zip(parts, c_refs):
        cv = c_ref[bb, GROUP_W:KV_W, :].astype(BF16)
        of = lax.dot_general(p_c, cv, NT_DIMS, preferred_element_type=F32)
        of = of + lax.dot_general(p_n, vnt, NT_DIMS, preferred_element_type=F32)
        outs.append(of / z)
        lses.append(m + jnp.log(z))
    top = jnp.maximum(jnp.maximum(lses[0], lses[1]), lses[2])
    es = [jnp.exp(l - top) for l in lses]
    den = es[0] + es[1] + es[2]
    for g in range(N_GROUPS):
        og = outs[g] * (es[g] / den)
        res = jnp.zeros((steps, GROUP_W), F32)
        for h in range(HEADS):
            res = jnp.where(_in_head(lane256, h), og[h * steps:(h + 1) * steps, :], res)
        o_ref[bb, :, g * GROUP_W:(g + 1) * GROUP_W] = res


def _post_kernel(*refs, tm, combine):
    if combine:
        (x_ref, o0_ref, o1_ref, o2_ref, l0_ref, l1_ref, l2_ref, co_ref,
         wo_ref, g2_ref, wu_ref, wd_ref, gf_ref, kn_ref, vn_ref, c2_ref, y_ref, n2_ref) = refs
        steps = kn_ref.shape[1]
        for bb in range(c2_ref.shape[0]):
            _shift_window(bb, c2_ref, n2_ref, _new_rows_transposed(kn_ref, bb, N_GROUPS - 1, steps),
                          _new_rows_transposed(vn_ref, bb, N_GROUPS - 1, steps), steps)
    else:
        x_ref, o_ref, co_ref, wo_ref, g2_ref, wu_ref, wd_ref, gf_ref, y_ref = refs
    halves = [pl.ds(k * (tm // 2), tm // 2) for k in range(2)]

    def mixer_input(rows):
        if not combine:
            return jnp.concatenate([o_ref[0, rows, :].astype(BF16), co_ref[0, rows, :]], axis=-1)
        slabs = lambda ref: jnp.concatenate([ref[0, 0, rows, :], ref[0, 1, rows, :]], axis=1)
        ls = [slabs(l_ref) for l_ref in (l0_ref, l1_ref, l2_ref)]
        top = jnp.maximum(jnp.maximum(ls[0], ls[1]), ls[2])
        es = [jnp.exp(l - top) for l in ls]
        den = es[0] + es[1] + es[2]
        parts = [(slabs(o_ref) * (e / den)).astype(BF16) for o_ref, e in zip((o0_ref, o1_ref, o2_ref), es)]
        return jnp.concatenate(parts + [co_ref[0, rows, :]], axis=-1)

    chunk = 1024
    n_chunks = wu_ref.shape[1] // chunk

    def mlp(xn):
        def up(c):
            hid = jnp.dot(xn, wu_ref[:, c * chunk:(c + 1) * chunk], preferred_element_type=F32)
            return jnp.square(jnp.maximum(hid, 0.0)).astype(BF16)

        acc = None
        hid_next = up(0)
        for c in range(n_chunks):
            hid = hid_next
            if c + 1 < n_chunks:
                hid_next = up(c + 1)
            part = jnp.dot(hid, wd_ref[c * chunk:(c + 1) * chunk, :], preferred_element_type=F32)
            acc = part if acc is None else acc + part
        return acc

    mixed = [mixer_input(rows) for rows in halves]
    x1 = [x_ref[0, rows, :] + jnp.dot(m, wo_ref[...], preferred_element_type=F32)
          for rows, m in zip(halves, mixed)]
    xn = [_rmsnorm(v, g2_ref[...]).astype(BF16) for v in x1]
    for rows, v, n in zip(halves, x1, xn):
        y_ref[0, rows, :] = _rmsnorm(v + mlp(n), gf_ref[...])


def _post(x, attn_inputs, co, w_out, g2, w_up, w_down, gf, combine, sample_shift=()):
    b, t, d_model = x.shape
    tm = min(t, POST_TM)
    assert t % tm == 0 and bool(sample_shift) == combine
    nt = t // tm

    def tok(a):
        if a.ndim == 4:
            return pl.BlockSpec((1, a.shape[1], tm, a.shape[3]), lambda bi, i: (bi, 0, i, 0))
        return pl.BlockSpec((1, tm, a.shape[2]), lambda bi, i: (bi, i, 0))

    const = lambda a: pl.BlockSpec(a.shape, lambda bi, i: (0, 0))
    out_specs, out_shape, shift_specs = [tok(x)], [jax.ShapeDtypeStruct((b, t, d_model), F32)], []
    if sample_shift:
        kn_s, vn_s, cache = sample_shift
        nb = cache.shape[0]
        bps = nb // (b * nt)
        assert bps * b * nt == nb
        per_step = lambda a: pl.BlockSpec((bps,) + a.shape[1:], lambda bi, i: (bi * nt + i, 0, 0))
        shift_specs = [per_step(kn_s), per_step(vn_s), per_step(cache)]
        out_specs.append(per_step(cache))
        out_shape.append(jax.ShapeDtypeStruct(cache.shape, F32))
    return pl.pallas_call(
        functools.partial(_post_kernel, tm=tm, combine=combine),
        grid=(b, nt),
        in_specs=[tok(x)] + [tok(a) for a in attn_inputs]
                 + [tok(co), const(w_out), const(g2), const(w_up), const(w_down), const(gf)] + shift_specs,
        out_specs=out_specs,
        out_shape=out_shape,
        compiler_params=pltpu.CompilerParams(
            dimension_semantics=("arbitrary", "arbitrary"), vmem_limit_bytes=VMEM_LIMIT),
        name="post_prompt" if combine else "post_sample",
    )(x, *attn_inputs, co, w_out, g2, w_up, w_down, gf, *sample_shift)


def kernel(x_prompt, x_sample, cache_kv_w128, cache_kv_w512, cache_kv_w2048, state_conv, norm_attn_g, w_in, conv_w, w_out, norm_mlp_g, w_up, w_down, norm_final_g):
    b, t, d_model = x_prompt.shape
    nb, steps, _ = x_sample.shape
    n_s = nb * steps
    assert w_in.shape[0] == 1, "single trunk layer"
    g1 = norm_attn_g[0][None, :]
    g2 = norm_mlp_g[0][None, :]
    gf = norm_final_g[None, :]
    w_in_b = w_in[0].astype(BF16)
    w_out_b = w_out[0].astype(BF16)
    w_up_b = w_up[0].astype(BF16)
    w_down_b = w_down[0].astype(BF16)
    cw = conv_w[0]

    caches = [jnp.transpose(c[0], (0, 2, 3, 4, 1)).reshape(nb, KV_W, w)
              for c, w in zip((cache_kv_w128, cache_kv_w512, cache_kv_w2048), WINDOWS)]
    state_pad = jnp.pad(state_conv[0], ((0, 0), (0, steps - (CONV_K - 1)), (0, 0))).reshape(n_s, CONV_CH)
    q_s, kn, vn, co_s, u_s = _proj_sample(x_sample.reshape(n_s, d_model), g1, w_in_b, cw, state_pad)
    r3 = lambda a: a.reshape(nb, steps, ATTN_W)
    qkv0, qkv1, qkv2, co, kv0, kv1, kv2, ctail, o_s, n0, n1 = _proj_prompt(
        x_prompt, g1, w_in_b, cw, r3(q_s), r3(kn), r3(vn), caches)
    (y_s,) = _post(x_sample.reshape(1, n_s, d_model), [o_s.reshape(1, n_s, ATTN_W)], co_s.reshape(1, n_s, CONV_CH),
                w_out_b, g2, w_up_b, w_down_b, gf, False)

    attn = [_attn_prompt(qkv, g) for g, qkv in enumerate((qkv0, qkv1, qkv2))]
    attn_inputs = [o for o, _ in attn] + [l for _, l in attn]
    y_p, n2 = _post(x_prompt, attn_inputs, co, w_out_b, g2, w_up_b, w_down_b, gf, True,
                    sample_shift=(r3(kn), r3(vn), caches[2]))

    kv_p = lambda a, w: a.reshape(1, b, w, 2, HEADS, HEAD_DIM)
    kv_s = lambda a, w: jnp.transpose(a.reshape(nb, 2, HEADS, HEAD_DIM, w), (0, 4, 1, 2, 3))[None]
    return (y_p,
            y_s.reshape(nb, steps, d_model),
            kv_p(kv0, WINDOWS[0]),
            kv_p(kv1, WINDOWS[1]),
            kv_p(kv2, WINDOWS[2]),
            ctail[:, SUBLANES - (CONV_K - 1):, :][None],
            kv_s(n0, WINDOWS[0]),
            kv_s(n1, WINDOWS[1]),
            kv_s(n2, WINDOWS[2]),
            u_s.reshape(nb, steps, CONV_CH)[:, steps - (CONV_K - 1):, :][None])
```

```python
import functools

import jax
import jax.numpy as jnp
from jax import lax
from jax.experimental import pallas as pl
from jax.experimental.pallas import tpu as pltpu

F32 = jnp.float32
BF16 = jnp.bfloat16

EPS = 1e-6
N_GROUPS = 3
HEADS = 4
HEAD_DIM = 64
GROUP_W = HEADS * HEAD_DIM
ATTN_W = N_GROUPS * GROUP_W
QKV_W = 3 * GROUP_W
CONV_CH = 256
CONV_K = 3
KV_W = 2 * GROUP_W
WINDOWS = (128, 512, 2048)
DILATIONS = (1, 4, 16)
SPAN = 128
QK_SCALE = HEAD_DIM ** -0.5
LOG2E = 1.4426950408889634
LN2 = 0.6931471805599453

_Q0, _K0, _V0, _B0, _C0, _H0, _END = 0, 768, 1536, 2304, 2560, 2816, 3072

SUBLANES = 8
LANES = 128
VMEM_LIMIT = 56 * 1024 * 1024

PROJ_TM = 512
POST_TM = 512
ATTN_SB = 4096
TILE = 128
NT_DIMS = (((1,), (1,)), ((), ()))


def _rmsnorm(x, g):
    y = x * lax.rsqrt(jnp.mean(x * x, axis=-1, keepdims=True) + EPS)
    return y * g


def _head_mask(h, dtype):
    lane = lax.broadcasted_iota(jnp.int32, (1, GROUP_W), 1)
    return ((lane >= h * HEAD_DIM) & (lane < (h + 1) * HEAD_DIM)).astype(F32).astype(dtype)


def _in_head(lane, h):
    return (lane >= h * HEAD_DIM) & (lane < (h + 1) * HEAD_DIM)


def _gated_conv(gate_b, u, prev1, prev2, cw_ref):
    y = cw_ref[0:1, :] * prev2
    y = y + cw_ref[1:2, :] * prev1
    y = y + cw_ref[2:3, :] * u
    return gate_b * y


def _band_masks(first):
    key = lax.broadcasted_iota(jnp.int32, (TILE + SPAN, TILE), 0)
    qry = lax.broadcasted_iota(jnp.int32, (TILE + SPAN, TILE), 1)
    band = (key >= qry) & (key <= qry + SPAN)
    return band, band & (key + jnp.where(first, 0, SPAN) >= SPAN)


def _tile_scores(q, kk):
    ss = []
    for h in range(0, HEADS, 2):
        q2 = jnp.concatenate([q * _head_mask(h, BF16), q * _head_mask(h + 1, BF16)], axis=0)
        s2 = lax.dot_general(kk, q2, NT_DIMS, preferred_element_type=F32)
        ss += [s2[:, 0:TILE], s2[:, TILE:2 * TILE]]
    return ss


def _tile_finish(ss, vv, valid):
    vt = vv.T
    o_t, l_t = [], []
    for h in range(HEADS):
        s = jnp.where(valid, ss[h], -jnp.inf)
        m = jnp.max(s, axis=0, keepdims=True)
        p = jnp.exp2(s - m)
        z = jnp.sum(p, axis=0, keepdims=True)
        acc = jnp.dot(vt[h * HEAD_DIM:(h + 1) * HEAD_DIM, :], p.astype(BF16),
                      preferred_element_type=F32)
        o_t.append(acc * (1.0 / z))
        l_t.append(jnp.broadcast_to(m * LN2 + jnp.log(z), (HEAD_DIM, TILE)))
    return jnp.concatenate(o_t, axis=0).T, jnp.concatenate(l_t, axis=0).T


def _proj_prompt_kernel(x_ref, g_ref, w_ref, cw_ref, qs_ref, kn_ref, vn_ref, c0_ref, c1_ref, c2_ref,
                        o0_ref, l0_ref, qkv1_ref, qkv2_ref, co_ref, kv0_ref, kv1_ref, kv2_ref, ct_ref,
                        os_ref, n0_ref, n1_ref,
                        carry_ref, slab_ref, kprev_ref, vprev_ref, *, tm, steps):
    i = pl.program_id(1)

    @pl.when(i == 0)
    def _():
        carry_ref[...] = jnp.zeros_like(carry_ref)

    @pl.when((i == 0) & (pl.program_id(0) == 0))
    def _():
        kprev_ref[...] = jnp.zeros_like(kprev_ref)
        vprev_ref[...] = jnp.zeros_like(vprev_ref)

    c_refs = (c0_ref, c1_ref, c2_ref)
    sample = [_sample_scores(bb, qs_ref, kn_ref, vn_ref, c_refs, (n0_ref, n1_ref), steps)
              for bb in range(qs_ref.shape[0])]

    xn = _rmsnorm(x_ref[0], g_ref[...]).astype(BF16)

    def proj(lo, hi):
        return jnp.dot(xn, w_ref[:, lo:hi], preferred_element_type=F32)

    qf = proj(_Q0, _K0) * (QK_SCALE * LOG2E)
    kf = proj(_K0, _V0)

    q0, k0 = (a[:, 0:GROUP_W].astype(BF16) for a in (qf, kf))
    slot = lax.rem(i, 2)
    band, band_first = _band_masks(i == 0)
    n_tiles = tm // TILE

    def keys_values(j, cur, prev_ref):
        if j == 0:
            return jnp.concatenate([prev_ref[slot], cur[0:TILE]], axis=0)
        return cur[(j - 1) * TILE:(j + 1) * TILE]

    scores = [_tile_scores(q0[j * TILE:(j + 1) * TILE], keys_values(j, k0, kprev_ref)) for j in range(n_tiles)]
    kprev_ref[1 - slot] = k0[tm - SPAN:tm]

    vf = proj(_V0, _B0)
    for g, (qkv_ref, kv_ref) in enumerate(((None, kv0_ref), (qkv1_ref, kv1_ref), (qkv2_ref, kv2_ref))):
        cols = slice(g * GROUP_W, (g + 1) * GROUP_W)
        rows = kv_ref.shape[1]
        kv_ref[0, :, 0:GROUP_W] = kf[tm - rows:tm, cols]
        kv_ref[0, :, GROUP_W:KV_W] = vf[tm - rows:tm, cols]
        d = DILATIONS[g]
        if d == 1:
            continue
        piece = jnp.concatenate([qf[:, cols], kf[:, cols], vf[:, cols]], axis=1)
        slabs = slab_ref.at[g - 1]
        for s in range(QKV_W // LANES):
            slabs[s] = piece[:, s * LANES:(s + 1) * LANES]
        for r in range(d):
            for s in range(QKV_W // LANES):
                qkv_ref[0, r, :, s * LANES:(s + 1) * LANES] = (
                    slabs[s, pl.ds(r, tm // d, stride=d), :].astype(BF16))

    v0 = vf[:, 0:GROUP_W].astype(BF16)
    for j in range(n_tiles):
        o_acc, l_acc = _tile_finish(scores[j], keys_values(j, v0, vprev_ref), band_first if j == 0 else band)
        for half in range(GROUP_W // LANES):
            lanes = slice(half * LANES, (half + 1) * LANES)
            o0_ref[0, half, j * TILE:(j + 1) * TILE, :] = o_acc[:, lanes]
            l0_ref[0, half, j * TILE:(j + 1) * TILE, :] = l_acc[:, lanes]
    vprev_ref[1 - slot] = v0[tm - SPAN:tm]

    for bb, parts in enumerate(sample):
        _sample_values(bb, parts, c_refs, os_ref, steps)

    gate_b = proj(_B0, _C0)
    u = proj(_C0, _H0) * proj(_H0, _END)
    row = lax.broadcasted_iota(jnp.int32, (tm, CONV_CH), 0)
    carry = carry_ref[...]
    prev1 = jnp.where(row == 0, carry[7:8, :], pltpu.roll(u, 1, 0))
    prev2 = jnp.where(row == 0, carry[6:7, :],
                      jnp.where(row == 1, carry[7:8, :], pltpu.roll(u, 2, 0)))
    co_ref[0] = _gated_conv(gate_b, u, prev1, prev2, cw_ref).astype(BF16)
    tail = u[tm - SUBLANES:tm, :]
    carry_ref[...] = tail
    ct_ref[0] = tail


def _proj_prompt(x, g, w_in, conv_w, q_s, kn_s, vn_s, caches):
    b, t, d_model = x.shape
    tm = PROJ_TM
    nt = t // tm
    assert t % tm == 0 and t >= WINDOWS[-1]
    nb, steps, _ = q_s.shape
    bps = nb // (b * nt)
    assert bps * b * nt == nb and steps == SUBLANES
    assert all(c.shape[1:] == (KV_W, w) for c, w in zip(caches, WINDOWS))
    stok = pl.BlockSpec((bps, steps, ATTN_W), lambda bi, i: (bi * nt + i, 0, 0))
    cspecs = [pl.BlockSpec((bps, KV_W, w), lambda bi, i: (bi * nt + i, 0, 0)) for w in WINDOWS]

    def tail_spec(w):
        rows = min(w, tm)
        nblk = max(w // tm, 1)
        return (pl.BlockSpec((1, rows, KV_W), lambda bi, i: (bi, jnp.maximum(i - (nt - nblk), 0), 0)),
                jax.ShapeDtypeStruct((b, w, KV_W), F32))

    tails = [tail_spec(w) for w in WINDOWS]
    tok = lambda width: pl.BlockSpec((1, tm, width), lambda bi, i: (bi, i, 0))
    const = lambda shape: pl.BlockSpec(shape, lambda bi, i: (0, 0))
    qkv_specs = [pl.BlockSpec((1, 2, tm, LANES), lambda bi, i: (bi, 0, i, 0))] * 2 + [
        pl.BlockSpec((1, d, tm // d, QKV_W), lambda bi, i: (bi, 0, i, 0)) for d in DILATIONS[1:]]
    qkv_shapes = [jax.ShapeDtypeStruct((b, 2, t, LANES), F32)] * 2 + [
        jax.ShapeDtypeStruct((b, d, t // d, QKV_W), BF16) for d in DILATIONS[1:]]
    assert DILATIONS[0] == 1 and tm % TILE == 0 and TILE == SPAN
    return pl.pallas_call(
        functools.partial(_proj_prompt_kernel, tm=tm, steps=steps),
        grid=(b, nt),
        in_specs=[tok(d_model), const((1, d_model)), const((d_model, _END)), const((CONV_K, CONV_CH)),
                  stok, stok, stok] + cspecs,
        out_specs=qkv_specs + [tok(CONV_CH)] + [s for s, _ in tails]
                  + [pl.BlockSpec((1, SUBLANES, CONV_CH), lambda bi, i: (bi, 0, 0)), stok] + cspecs[:2],
        out_shape=qkv_shapes + [jax.ShapeDtypeStruct((b, t, CONV_CH), BF16)] + [s for _, s in tails]
                  + [jax.ShapeDtypeStruct((b, SUBLANES, CONV_CH), F32),
                     jax.ShapeDtypeStruct((nb, steps, ATTN_W), F32)]
                  + [jax.ShapeDtypeStruct((nb, KV_W, w), F32) for w in WINDOWS[:2]],
        scratch_shapes=[pltpu.VMEM((SUBLANES, CONV_CH), F32),
                        pltpu.VMEM((N_GROUPS - 1, QKV_W // LANES, tm, LANES), F32),
                        pltpu.VMEM((2, SPAN, GROUP_W), BF16), pltpu.VMEM((2, SPAN, GROUP_W), BF16)],
        compiler_params=pltpu.CompilerParams(
            dimension_semantics=("arbitrary", "arbitrary"), vmem_limit_bytes=VMEM_LIMIT),
        name="proj_prompt",
    )(x, g, w_in, conv_w, q_s, kn_s, vn_s, *caches)


def _proj_sample_kernel(x_ref, g_ref, w_ref, cw_ref, sp_ref,
                        q_ref, kn_ref, vn_ref, co_ref, u_ref, *, tm):
    xn = _rmsnorm(x_ref[...], g_ref[...]).astype(BF16)

    def proj(lo, hi):
        return jnp.dot(xn, w_ref[:, lo:hi], preferred_element_type=F32)

    q_ref[...] = proj(_Q0, _K0) * QK_SCALE
    kn_ref[...] = proj(_K0, _V0)
    vn_ref[...] = proj(_V0, _B0)
    gate_b = proj(_B0, _C0)
    u = proj(_C0, _H0) * proj(_H0, _END)
    step = lax.broadcasted_iota(jnp.int32, (tm, CONV_CH), 0) & (SUBLANES - 1)
    sp = sp_ref[...]
    prev1 = jnp.where(step == 0, pltpu.roll(sp, tm - 1, 0), pltpu.roll(u, 1, 0))
    prev2 = jnp.where(step < 2, sp, pltpu.roll(u, 2, 0))
    co_ref[...] = _gated_conv(gate_b, u, prev1, prev2, cw_ref).astype(BF16)
    u_ref[...] = u


def _proj_sample(x, g, w_in, conv_w, state_pad):
    n, d_model = x.shape
    tm = min(n, POST_TM)
    assert n % tm == 0 and tm % SUBLANES == 0
    tok = lambda width: pl.BlockSpec((tm, width), lambda i: (i, 0))
    const = lambda shape: pl.BlockSpec(shape, lambda i: (0, 0))
    return pl.pallas_call(
        functools.partial(_proj_sample_kernel, tm=tm),
        grid=(n // tm,),
        in_specs=[tok(d_model), const((1, d_model)), const((d_model, _END)), const((CONV_K, CONV_CH)),
                  tok(CONV_CH)],
        out_specs=[tok(ATTN_W), tok(ATTN_W), tok(ATTN_W), tok(CONV_CH), tok(CONV_CH)],
        out_shape=[jax.ShapeDtypeStruct((n, ATTN_W), F32)] * 3
                  + [jax.ShapeDtypeStruct((n, CONV_CH), BF16), jax.ShapeDtypeStruct((n, CONV_CH), F32)],
        compiler_params=pltpu.CompilerParams(
            dimension_semantics=("arbitrary",), vmem_limit_bytes=VMEM_LIMIT),
        name="proj_sample",
    )(x, g, w_in, conv_w, state_pad)


def _attn_prompt_kernel(qkv_ref, kp_ref, vp_ref, o_ref, l_ref, *, d, n_tiles):
    band, band_first = _band_masks(pl.program_id(1) == 0)
    k_cols = slice(GROUP_W, 2 * GROUP_W)
    v_cols = slice(2 * GROUP_W, 3 * GROUP_W)

    def keys_values(r, j, cols, prev_ref):
        if j == 0:
            return jnp.concatenate([prev_ref[0, r], qkv_ref[0, r, 0:TILE, cols]], axis=0)
        return qkv_ref[0, r, (j - 1) * TILE:(j + 1) * TILE, cols]

    def scores(r, j):
        return _tile_scores(qkv_ref[0, r, j * TILE:(j + 1) * TILE, 0:GROUP_W], keys_values(r, j, k_cols, kp_ref))

    def finish(r, j, ss):
        o_acc, l_acc = _tile_finish(ss, keys_values(r, j, v_cols, vp_ref), band_first if j == 0 else band)
        if d == 1:
            rows = pl.ds(j * TILE, TILE)
        else:
            rows = pl.ds(j * TILE * d + r, TILE, stride=d)
        for half in range(GROUP_W // LANES):
            lanes = slice(half * LANES, (half + 1) * LANES)
            o_ref[0, half, rows, :] = o_acc[:, lanes]
            l_ref[0, half, rows, :] = l_acc[:, lanes]

    units = [(r, j) for r in range(d) for j in range(n_tiles)]
    ss_next = scores(*units[0])
    for n, unit in enumerate(units):
        ss = ss_next
        if n + 1 < len(units):
            ss_next = scores(*units[n + 1])
        finish(*unit, ss)


def _attn_prompt(qkv, g):
    b, d, tc, _ = qkv.shape
    t = tc * d
    sb = min(ATTN_SB, t)
    n_tiles = sb // d // TILE
    assert d == DILATIONS[g] and t % sb == 0 and sb % (d * TILE) == 0 and TILE == SPAN

    def prev_spec(col_block):
        return pl.BlockSpec((1, d, SPAN, GROUP_W),
                            lambda bi, s: (bi, 0, jnp.maximum(s * n_tiles - 1, 0), col_block))

    return pl.pallas_call(
        functools.partial(_attn_prompt_kernel, d=d, n_tiles=n_tiles),
        grid=(b, t // sb),
        in_specs=[pl.BlockSpec((1, d, sb // d, QKV_W), lambda bi, s: (bi, 0, s, 0)),
                  prev_spec(1), prev_spec(2)],
        out_specs=[pl.BlockSpec((1, 2, sb, LANES), lambda bi, s: (bi, 0, s, 0))] * 2,
        out_shape=[jax.ShapeDtypeStruct((b, 2, t, LANES), F32)] * 2,
        compiler_params=pltpu.CompilerParams(
            dimension_semantics=("arbitrary", "arbitrary"), vmem_limit_bytes=VMEM_LIMIT),
        name=f"attn_prompt_g{g}",
    )(qkv, qkv, qkv)


def _new_rows_transposed(new_ref, bb, g, steps):
    new = new_ref[bb, :, g * GROUP_W:(g + 1) * GROUP_W]
    return jnp.concatenate([new, jnp.zeros((LANES - steps, GROUP_W), F32)], axis=0).T


def _shift_window(bb, c_ref, n_ref, knt, vnt, steps):
    w = c_ref.shape[2]
    shift = LANES - steps
    lane_kv = lax.broadcasted_iota(jnp.int32, (KV_W, LANES), 1)
    nxt = pltpu.roll(c_ref[bb, :, 0:LANES], shift, 1)
    for cb in range(w // LANES):
        cur = nxt
        if cb + 1 < w // LANES:
            nxt = pltpu.roll(c_ref[bb, :, (cb + 1) * LANES:(cb + 2) * LANES], shift, 1)
        else:
            nxt = pltpu.roll(jnp.concatenate([knt, vnt], axis=0), shift, 1)
        n_ref[bb, :, cb * LANES:(cb + 1) * LANES] = jnp.where(lane_kv < shift, cur, nxt)


def _sample_scores(bb, q_ref, kn_ref, vn_ref, c_refs, n_refs, steps):
    rows = HEADS * steps
    step_of = lambda shape: lax.broadcasted_iota(jnp.int32, shape, 0) & (steps - 1)
    parts = []
    for g, c_ref in enumerate(c_refs):
        w = c_ref.shape[2]
        d = DILATIONS[g]
        knt = _new_rows_transposed(kn_ref, bb, g, steps)
        vnt = _new_rows_transposed(vn_ref, bb, g, steps)
        if g < len(n_refs):
            _shift_window(bb, c_ref, n_refs[g], knt, vnt, steps)

        qg = q_ref[bb, :, g * GROUP_W:(g + 1) * GROUP_W]
        qs = jnp.concatenate([qg * _head_mask(h, F32) for h in range(HEADS)], axis=0).astype(BF16)
        ck = c_ref[bb, 0:GROUP_W, :].astype(BF16)
        s_c = jnp.dot(qs, ck, preferred_element_type=F32)
        s_n = jnp.dot(qs, knt.astype(BF16), preferred_element_type=F32)
        t_c = step_of((rows, w))
        i_c = lax.broadcasted_iota(jnp.int32, (rows, w), 1)
        valid_c = (i_c >= t_c) & (((i_c - t_c) & (d - 1)) == 0)
        t_n = step_of((rows, LANES))
        i_n = lax.broadcasted_iota(jnp.int32, (rows, LANES), 1)
        valid_n = (i_n <= t_n) & (((t_n - i_n) & (d - 1)) == 0)
        s_c = jnp.where(valid_c, s_c, -jnp.inf)
        s_n = jnp.where(valid_n, s_n, -jnp.inf)
        m = jnp.maximum(jnp.max(s_c, axis=-1, keepdims=True), jnp.max(s_n, axis=-1, keepdims=True))
        p_c = jnp.exp(s_c - m)
        p_n = jnp.exp(s_n - m)
        z = jnp.sum(p_c, axis=-1, keepdims=True) + jnp.sum(p_n, axis=-1, keepdims=True)
        parts.append((p_c.astype(BF16), p_n.astype(BF16), z, m, vnt.astype(BF16)))
    return parts


def _sample_values(bb, parts, c_refs, o_ref, steps):
    lane256 = lax.broadcasted_iota(jnp.int32, (steps, GROUP_W), 1)
    outs, lses = [], []
    for (p_c, p_n, z, m, vnt), c_ref in zip(parts, c_refs):
        cv = c_ref[bb, GROUP_W:KV_W, :].astype(BF16)
        of = lax.dot_general(p_c, cv, NT_DIMS, preferred_element_type=F32)
        of = of + lax.dot_general(p_n, vnt, NT_DIMS, preferred_element_type=F32)
        outs.append(of / z)
        lses.append(m + jnp.log(z))
    top = jnp.maximum(jnp.maximum(lses[0], lses[1]), lses[2])
    es = [jnp.exp(l - top) for l in lses]
    den = es[0] + es[1] + es[2]
    for g in range(N_GROUPS):
        og = outs[g] * (es[g] / den)
        res = jnp.zeros((steps, GROUP_W), F32)
        for h in range(HEADS):
            res = jnp.where(_in_head(lane256, h), og[h * steps:(h + 1) * steps, :], res)
        o_ref[bb, :, g * GROUP_W:(g + 1) * GROUP_W] = res


def _post_kernel(*refs, tm, combine):
    if combine:
        (x_ref, o0_ref, o1_ref, o2_ref, l0_ref, l1_ref, l2_ref, co_ref,
         wo_ref, g2_ref, wu_ref, wd_ref, gf_ref, kn_ref, vn_ref, c2_ref, y_ref, n2_ref) = refs
        steps = kn_ref.shape[1]
        for bb in range(c2_ref.shape[0]):
            _shift_window(bb, c2_ref, n2_ref, _new_rows_transposed(kn_ref, bb, N_GROUPS - 1, steps),
                          _new_rows_transposed(vn_ref, bb, N_GROUPS - 1, steps), steps)
    else:
        x_ref, o_ref, co_ref, wo_ref, g2_ref, wu_ref, wd_ref, gf_ref, y_ref = refs
    halves = [pl.ds(k * (tm // 2), tm // 2) for k in range(2)]

    def mixer_input(rows):
        if not combine:
            return jnp.concatenate([o_ref[0, rows, :].astype(BF16), co_ref[0, rows, :]], axis=-1)
        slabs = lambda ref: jnp.concatenate([ref[0, 0, rows, :], ref[0, 1, rows, :]], axis=1)
        ls = [slabs(l_ref) for l_ref in (l0_ref, l1_ref, l2_ref)]
        top = jnp.maximum(jnp.maximum(ls[0], ls[1]), ls[2])
        es = [jnp.exp(l - top) for l in ls]
        den = es[0] + es[1] + es[2]
        parts = [(slabs(o_ref) * (e / den)).astype(BF16) for o_ref, e in zip((o0_ref, o1_ref, o2_ref), es)]
        return jnp.concatenate(parts + [co_ref[0, rows, :]], axis=-1)

    chunk = 1024
    n_chunks = wu_ref.shape[1] // chunk

    def mlp(xn):
        def up(c):
            hid = jnp.dot(xn, wu_ref[:, c * chunk:(c + 1) * chunk], preferred_element_type=F32)
            return jnp.square(jnp.maximum(hid, 0.0)).astype(BF16)

        acc = None
        hid_next = up(0)
        for c in range(n_chunks):
            hid = hid_next
            if c + 1 < n_chunks:
                hid_next = up(c + 1)
            part = jnp.dot(hid, wd_ref[c * chunk:(c + 1) * chunk, :], preferred_element_type=F32)
            acc = part if acc is None else acc + part
        return acc

    mixed = [mixer_input(rows) for rows in halves]
    x1 = [x_ref[0, rows, :] + jnp.dot(m, wo_ref[...], preferred_element_type=F32)
          for rows, m in zip(halves, mixed)]
    xn = [_rmsnorm(v, g2_ref[...]).astype(BF16) for v in x1]
    for rows, v, n in zip(halves, x1, xn):
        y_ref[0, rows, :] = _rmsnorm(v + mlp(n), gf_ref[...])


def _post(x, attn_inputs, co, w_out, g2, w_up, w_down, gf, combine, sample_shift=()):
    b, t, d_model = x.shape
    tm = min(t, POST_TM)
    assert t % tm == 0 and bool(sample_shift) == combine
    nt = t // tm

    def tok(a):
        if a.ndim == 4:
            return pl.BlockSpec((1, a.shape[1], tm, a.shape[3]), lambda bi, i: (bi, 0, i, 0))
        return pl.BlockSpec((1, tm, a.shape[2]), lambda bi, i: (bi, i, 0))

    const = lambda a: pl.BlockSpec(a.shape, lambda bi, i: (0, 0))
    out_specs, out_shape, shift_specs = [tok(x)], [jax.ShapeDtypeStruct((b, t, d_model), F32)], []
    if sample_shift:
        kn_s, vn_s, cache = sample_shift
        nb = cache.shape[0]
        bps = nb // (b * nt)
        assert bps * b * nt == nb
        per_step = lambda a: pl.BlockSpec((bps,) + a.shape[1:], lambda bi, i: (bi * nt + i, 0, 0))
        shift_specs = [per_step(kn_s), per_step(vn_s), per_step(cache)]
        out_specs.append(per_step(cache))
        out_shape.append(jax.ShapeDtypeStruct(cache.shape, F32))
    return pl.pallas_call(
        functools.partial(_post_kernel, tm=tm, combine=combine),
        grid=(b, nt),
        in_specs=[tok(x)] + [tok(a) for a in attn_inputs]
                 + [tok(co), const(w_out), const(g2), const(w_up), const(w_down), const(gf)] + shift_specs,
        out_specs=out_specs,
        out_shape=out_shape,
        compiler_params=pltpu.CompilerParams(
            dimension_semantics=("arbitrary", "arbitrary"), vmem_limit_bytes=VMEM_LIMIT),
        name="post_prompt" if combine else "post_sample",
    )(x, *attn_inputs, co, w_out, g2, w_up, w_down, gf, *sample_shift)


def kernel(x_prompt, x_sample, cache_kv_w128, cache_kv_w512, cache_kv_w2048, state_conv, norm_attn_g, w_in, conv_w, w_out, norm_mlp_g, w_up, w_down, norm_final_g):
    b, t, d_model = x_prompt.shape
    nb, steps, _ = x_sample.shape
    n_s = nb * steps
    assert w_in.shape[0] == 1, "single trunk layer"
    g1 = norm_attn_g[0][None, :]
    g2 = norm_mlp_g[0][None, :]
    gf = norm_final_g[None, :]
    w_in_b = w_in[0].astype(BF16)
    w_out_b = w_out[0].astype(BF16)
    w_up_b = w_up[0].astype(BF16)
    w_down_b = w_down[0].astype(BF16)
    cw = conv_w[0]

    caches = [jnp.transpose(c[0], (0, 2, 3, 4, 1)).reshape(nb, KV_W, w)
              for c, w in zip((cache_kv_w128, cache_kv_w512, cache_kv_w2048), WINDOWS)]
    state_pad = jnp.pad(state_conv[0], ((0, 0), (0, steps - (CONV_K - 1)), (0, 0))).reshape(n_s, CONV_CH)
    q_s, kn, vn, co_s, u_s = _proj_sample(x_sample.reshape(n_s, d_model), g1, w_in_b, cw, state_pad)
    r3 = lambda a: a.reshape(nb, steps, ATTN_W)
    o0, l0, qkv1, qkv2, co, kv0, kv1, kv2, ctail, o_s, n0, n1 = _proj_prompt(
        x_prompt, g1, w_in_b, cw, r3(q_s), r3(kn), r3(vn), caches)
    (y_s,) = _post(x_sample.reshape(1, n_s, d_model), [o_s.reshape(1, n_s, ATTN_W)], co_s.reshape(1, n_s, CONV_CH),
                w_out_b, g2, w_up_b, w_down_b, gf, False)

    attn = [(o0, l0)] + [_attn_prompt(qkv, g) for g, qkv in ((1, qkv1), (2, qkv2))]
    attn_inputs = [o for o, _ in attn] + [l for _, l in attn]
    y_p, n2 = _post(x_prompt, attn_inputs, co, w_out_b, g2, w_up_b, w_down_b, gf, True,
                    sample_shift=(r3(kn), r3(vn), caches[2]))

    kv_p = lambda a, w: a.reshape(1, b, w, 2, HEADS, HEAD_DIM)
    kv_s = lambda a, w: jnp.transpose(a.reshape(nb, 2, HEADS, HEAD_DIM, w), (0, 4, 1, 2, 3))[None]
    return (y_p,
            y_s.reshape(nb, steps, d_model),
            kv_p(kv0, WINDOWS[0]),
            kv_p(kv1, WINDOWS[1]),
            kv_p(kv2, WINDOWS[2]),
            ctail[:, SUBLANES - (CONV_K - 1):, :][None],
            kv_s(n0, WINDOWS[0]),
            kv_s(n1, WINDOWS[1]),
            kv_s(n2, WINDOWS[2]),
            u_s.reshape(nb, steps, CONV_CH)[:, steps - (CONV_K - 1):, :][None])
```

```python
import functools

import jax
import jax.numpy as jnp
from jax import lax
from jax.experimental import pallas as pl
from jax.experimental.pallas import tpu as pltpu

F32 = jnp.float32
BF16 = jnp.bfloat16

EPS = 1e-6
N_GROUPS = 3
HEADS = 4
HEAD_DIM = 64
GROUP_W = HEADS * HEAD_DIM
ATTN_W = N_GROUPS * GROUP_W
QKV_W = 3 * GROUP_W
CONV_CH = 256
CONV_K = 3
KV_W = 2 * GROUP_W
WINDOWS = (128, 512, 2048)
DILATIONS = (1, 4, 16)
SPAN = 128
QK_SCALE = HEAD_DIM ** -0.5
LOG2E = 1.4426950408889634
LN2 = 0.6931471805599453

_Q0, _K0, _V0, _B0, _C0, _H0, _END = 0, 768, 1536, 2304, 2560, 2816, 3072

SUBLANES = 8
LANES = 128
VMEM_LIMIT = 56 * 1024 * 1024

PROJ_TM = 512
POST_TM = 512
ATTN_SB = 4096
TILE = 128
NT_DIMS = (((1,), (1,)), ((), ()))


def _rmsnorm(x, g):
    y = x * lax.rsqrt(jnp.mean(x * x, axis=-1, keepdims=True) + EPS)
    return y * g


def _head_mask(h, dtype):
    lane = lax.broadcasted_iota(jnp.int32, (1, GROUP_W), 1)
    return ((lane >= h * HEAD_DIM) & (lane < (h + 1) * HEAD_DIM)).astype(F32).astype(dtype)


def _in_head(lane, h):
    return (lane >= h * HEAD_DIM) & (lane < (h + 1) * HEAD_DIM)


def _gated_conv(gate_b, u, prev1, prev2, cw_ref):
    y = cw_ref[0:1, :] * prev2
    y = y + cw_ref[1:2, :] * prev1
    y = y + cw_ref[2:3, :] * u
    return gate_b * y


def _band_masks(first):
    key = lax.broadcasted_iota(jnp.int32, (TILE + SPAN, TILE), 0)
    qry = lax.broadcasted_iota(jnp.int32, (TILE + SPAN, TILE), 1)
    band = (key >= qry) & (key <= qry + SPAN)
    return band, band & (key + jnp.where(first, 0, SPAN) >= SPAN)


def _tile_scores(q, kk):
    ss = []
    for h in range(0, HEADS, 2):
        q2 = jnp.concatenate([q * _head_mask(h, BF16), q * _head_mask(h + 1, BF16)], axis=0)
        s2 = lax.dot_general(kk, q2, NT_DIMS, preferred_element_type=F32)
        ss += [s2[:, 0:TILE], s2[:, TILE:2 * TILE]]
    return ss


def _tile_finish(ss, vv, valid):
    vt = vv.T
    o_t, l_t = [], []
    for h in range(HEADS):
        s = jnp.where(valid, ss[h], -jnp.inf)
        m = jnp.max(s, axis=0, keepdims=True)
        p = jnp.exp2(s - m)
        z = jnp.sum(p, axis=0, keepdims=True)
        acc = jnp.dot(vt[h * HEAD_DIM:(h + 1) * HEAD_DIM, :], p.astype(BF16),
                      preferred_element_type=F32)
        o_t.append(acc * (1.0 / z))
        l_t.append(jnp.broadcast_to(m * LN2 + jnp.log(z), (HEAD_DIM, TILE)))
    return jnp.concatenate(o_t, axis=0).T, jnp.concatenate(l_t, axis=0).T


def _proj_prompt_kernel(x_ref, g_ref, w_ref, cw_ref, qs_ref, kn_ref, vn_ref, c0_ref, c1_ref, c2_ref,
                        o0_ref, l0_ref, o1_ref, l1_ref, qkv2_ref, co_ref, kv0_ref, kv1_ref, kv2_ref, ct_ref,
                        os_ref, n0_ref, n1_ref,
                        carry_ref, slab_ref, kprev_ref, vprev_ref, k1prev_ref, v1prev_ref, *, tm, steps):
    i = pl.program_id(1)

    @pl.when(i == 0)
    def _():
        carry_ref[...] = jnp.zeros_like(carry_ref)

    @pl.when((i == 0) & (pl.program_id(0) == 0))
    def _():
        kprev_ref[...] = jnp.zeros_like(kprev_ref)
        vprev_ref[...] = jnp.zeros_like(vprev_ref)
        k1prev_ref[...] = jnp.zeros_like(k1prev_ref)
        v1prev_ref[...] = jnp.zeros_like(v1prev_ref)

    c_refs = (c0_ref, c1_ref, c2_ref)
    sample = [_sample_scores(bb, qs_ref, kn_ref, vn_ref, c_refs, (n0_ref, n1_ref), steps)
              for bb in range(qs_ref.shape[0])]

    xn = _rmsnorm(x_ref[0], g_ref[...]).astype(BF16)

    def proj(lo, hi):
        return jnp.dot(xn, w_ref[:, lo:hi], preferred_element_type=F32)

    qf = proj(_Q0, _K0) * (QK_SCALE * LOG2E)
    kf = proj(_K0, _V0)

    q0, k0 = (a[:, 0:GROUP_W].astype(BF16) for a in (qf, kf))
    slot = lax.rem(i, 2)
    band, band_first = _band_masks(i == 0)
    n_tiles = tm // TILE

    def keys_values(j, cur, prev_ref):
        if j == 0:
            return jnp.concatenate([prev_ref[slot], cur[0:TILE]], axis=0)
        return cur[(j - 1) * TILE:(j + 1) * TILE]

    scores = [_tile_scores(q0[j * TILE:(j + 1) * TILE], keys_values(j, k0, kprev_ref)) for j in range(n_tiles)]
    kprev_ref[1 - slot] = k0[tm - SPAN:tm]

    d1 = DILATIONS[1]
    g1_cols = slice(GROUP_W, 2 * GROUP_W)

    def regroup(x, part):
        for half in range(GROUP_W // LANES):
            slab_ref[0, 2 * part + half] = x[:, half * LANES:(half + 1) * LANES]
        return [jnp.concatenate([slab_ref[0, 2 * part + half, pl.ds(r, TILE, stride=d1), :]
                                 for half in range(GROUP_W // LANES)], axis=1).astype(BF16) for r in range(d1)]

    q1, k1 = regroup(qf[:, g1_cols], 0), regroup(kf[:, g1_cols], 1)
    scores1 = [_tile_scores(q1[r], jnp.concatenate([k1prev_ref[slot, r], k1[r]], axis=0)) for r in range(d1)]
    for r in range(d1):
        k1prev_ref[1 - slot, r] = k1[r]

    vf = proj(_V0, _B0)
    for g, (qkv_ref, kv_ref) in enumerate(((None, kv0_ref), (None, kv1_ref), (qkv2_ref, kv2_ref))):
        cols = slice(g * GROUP_W, (g + 1) * GROUP_W)
        rows = kv_ref.shape[1]
        kv_ref[0, :, 0:GROUP_W] = kf[tm - rows:tm, cols]
        kv_ref[0, :, GROUP_W:KV_W] = vf[tm - rows:tm, cols]
        d = DILATIONS[g]
        if qkv_ref is None:
            continue
        piece = jnp.concatenate([qf[:, cols], kf[:, cols], vf[:, cols]], axis=1)
        slabs = slab_ref.at[g - 1]
        for s in range(QKV_W // LANES):
            slabs[s] = piece[:, s * LANES:(s + 1) * LANES]
        for r in range(d):
            for s in range(QKV_W // LANES):
                qkv_ref[0, r, :, s * LANES:(s + 1) * LANES] = (
                    slabs[s, pl.ds(r, tm // d, stride=d), :].astype(BF16))

    v0 = vf[:, 0:GROUP_W].astype(BF16)
    for j in range(n_tiles):
        o_acc, l_acc = _tile_finish(scores[j], keys_values(j, v0, vprev_ref), band_first if j == 0 else band)
        for half in range(GROUP_W // LANES):
            lanes = slice(half * LANES, (half + 1) * LANES)
            o0_ref[0, half, j * TILE:(j + 1) * TILE, :] = o_acc[:, lanes]
            l0_ref[0, half, j * TILE:(j + 1) * TILE, :] = l_acc[:, lanes]
    vprev_ref[1 - slot] = v0[tm - SPAN:tm]

    v1 = regroup(vf[:, g1_cols], 2)
    for r in range(d1):
        o_acc, l_acc = _tile_finish(scores1[r], jnp.concatenate([v1prev_ref[slot, r], v1[r]], axis=0), band_first)
        for half in range(GROUP_W // LANES):
            lanes = slice(half * LANES, (half + 1) * LANES)
            o1_ref[0, half, pl.ds(r, TILE, stride=d1), :] = o_acc[:, lanes]
            l1_ref[0, half, pl.ds(r, TILE, stride=d1), :] = l_acc[:, lanes]
        v1prev_ref[1 - slot, r] = v1[r]

    for bb, parts in enumerate(sample):
        _sample_values(bb, parts, c_refs, os_ref, steps)

    gate_b = proj(_B0, _C0)
    u = proj(_C0, _H0) * proj(_H0, _END)
    row = lax.broadcasted_iota(jnp.int32, (tm, CONV_CH), 0)
    carry = carry_ref[...]
    prev1 = jnp.where(row == 0, carry[7:8, :], pltpu.roll(u, 1, 0))
    prev2 = jnp.where(row == 0, carry[6:7, :],
                      jnp.where(row == 1, carry[7:8, :], pltpu.roll(u, 2, 0)))
    co_ref[0] = _gated_conv(gate_b, u, prev1, prev2, cw_ref).astype(BF16)
    tail = u[tm - SUBLANES:tm, :]
    carry_ref[...] = tail
    ct_ref[0] = tail


def _proj_prompt(x, g, w_in, conv_w, q_s, kn_s, vn_s, caches):
    b, t, d_model = x.shape
    tm = PROJ_TM
    nt = t // tm
    assert t % tm == 0 and t >= WINDOWS[-1]
    nb, steps, _ = q_s.shape
    bps = nb // (b * nt)
    assert bps * b * nt == nb and steps == SUBLANES
    assert all(c.shape[1:] == (KV_W, w) for c, w in zip(caches, WINDOWS))
    stok = pl.BlockSpec((bps, steps, ATTN_W), lambda bi, i: (bi * nt + i, 0, 0))
    cspecs = [pl.BlockSpec((bps, KV_W, w), lambda bi, i: (bi * nt + i, 0, 0)) for w in WINDOWS]

    def tail_spec(w):
        rows = min(w, tm)
        nblk = max(w // tm, 1)
        return (pl.BlockSpec((1, rows, KV_W), lambda bi, i: (bi, jnp.maximum(i - (nt - nblk), 0), 0)),
                jax.ShapeDtypeStruct((b, w, KV_W), F32))

    tails = [tail_spec(w) for w in WINDOWS]
    tok = lambda width: pl.BlockSpec((1, tm, width), lambda bi, i: (bi, i, 0))
    const = lambda shape: pl.BlockSpec(shape, lambda bi, i: (0, 0))
    qkv_specs = [pl.BlockSpec((1, 2, tm, LANES), lambda bi, i: (bi, 0, i, 0))] * 4 + [
        pl.BlockSpec((1, d, tm // d, QKV_W), lambda bi, i: (bi, 0, i, 0)) for d in DILATIONS[2:]]
    qkv_shapes = [jax.ShapeDtypeStruct((b, 2, t, LANES), F32)] * 4 + [
        jax.ShapeDtypeStruct((b, d, t // d, QKV_W), BF16) for d in DILATIONS[2:]]
    assert DILATIONS[0] == 1 and tm % TILE == 0 and TILE == SPAN and tm // DILATIONS[1] == TILE
    return pl.pallas_call(
        functools.partial(_proj_prompt_kernel, tm=tm, steps=steps),
        grid=(b, nt),
        in_specs=[tok(d_model), const((1, d_model)), const((d_model, _END)), const((CONV_K, CONV_CH)),
                  stok, stok, stok] + cspecs,
        out_specs=qkv_specs + [tok(CONV_CH)] + [s for s, _ in tails]
                  + [pl.BlockSpec((1, SUBLANES, CONV_CH), lambda bi, i: (bi, 0, 0)), stok] + cspecs[:2],
        out_shape=qkv_shapes + [jax.ShapeDtypeStruct((b, t, CONV_CH), BF16)] + [s for _, s in tails]
                  + [jax.ShapeDtypeStruct((b, SUBLANES, CONV_CH), F32),
                     jax.ShapeDtypeStruct((nb, steps, ATTN_W), F32)]
                  + [jax.ShapeDtypeStruct((nb, KV_W, w), F32) for w in WINDOWS[:2]],
        scratch_shapes=[pltpu.VMEM((SUBLANES, CONV_CH), F32),
                        pltpu.VMEM((N_GROUPS - 1, QKV_W // LANES, tm, LANES), F32),
                        pltpu.VMEM((2, SPAN, GROUP_W), BF16), pltpu.VMEM((2, SPAN, GROUP_W), BF16),
                        pltpu.VMEM((2, DILATIONS[1], SPAN, GROUP_W), BF16),
                        pltpu.VMEM((2, DILATIONS[1], SPAN, GROUP_W), BF16)],
        compiler_params=pltpu.CompilerParams(
            dimension_semantics=("arbitrary", "arbitrary"), vmem_limit_bytes=VMEM_LIMIT),
        name="proj_prompt",
    )(x, g, w_in, conv_w, q_s, kn_s, vn_s, *caches)


def _proj_sample_kernel(x_ref, g_ref, w_ref, cw_ref, sp_ref,
                        q_ref, kn_ref, vn_ref, co_ref, u_ref, *, tm):
    xn = _rmsnorm(x_ref[...], g_ref[...]).astype(BF16)

    def proj(lo, hi):
        return jnp.dot(xn, w_ref[:, lo:hi], preferred_element_type=F32)

    q_ref[...] = proj(_Q0, _K0) * QK_SCALE
    kn_ref[...] = proj(_K0, _V0)
    vn_ref[...] = proj(_V0, _B0)
    gate_b = proj(_B0, _C0)
    u = proj(_C0, _H0) * proj(_H0, _END)
    step = lax.broadcasted_iota(jnp.int32, (tm, CONV_CH), 0) & (SUBLANES - 1)
    sp = sp_ref[...]
    prev1 = jnp.where(step == 0, pltpu.roll(sp, tm - 1, 0), pltpu.roll(u, 1, 0))
    prev2 = jnp.where(step < 2, sp, pltpu.roll(u, 2, 0))
    co_ref[...] = _gated_conv(gate_b, u, prev1, prev2, cw_ref).astype(BF16)
    u_ref[...] = u


def _proj_sample(x, g, w_in, conv_w, state_pad):
    n, d_model = x.shape
    tm = min(n, POST_TM)
    assert n % tm == 0 and tm % SUBLANES == 0
    tok = lambda width: pl.BlockSpec((tm, width), lambda i: (i, 0))
    const = lambda shape: pl.BlockSpec(shape, lambda i: (0, 0))
    return pl.pallas_call(
        functools.partial(_proj_sample_kernel, tm=tm),
        grid=(n // tm,),
        in_specs=[tok(d_model), const((1, d_model)), const((d_model, _END)), const((CONV_K, CONV_CH)),
                  tok(CONV_CH)],
        out_specs=[tok(ATTN_W), tok(ATTN_W), tok(ATTN_W), tok(CONV_CH), tok(CONV_CH)],
        out_shape=[jax.ShapeDtypeStruct((n, ATTN_W), F32)] * 3
                  + [jax.ShapeDtypeStruct((n, CONV_CH), BF16), jax.ShapeDtypeStruct((n, CONV_CH), F32)],
        compiler_params=pltpu.CompilerParams(
            dimension_semantics=("arbitrary",), vmem_limit_bytes=VMEM_LIMIT),
        name="proj_sample",
    )(x, g, w_in, conv_w, state_pad)


def _attn_prompt_kernel(qkv_ref, kp_ref, vp_ref, o_ref, l_ref, *, d, n_tiles):
    band, band_first = _band_masks(pl.program_id(1) == 0)
    k_cols = slice(GROUP_W, 2 * GROUP_W)
    v_cols = slice(2 * GROUP_W, 3 * GROUP_W)

    def keys_values(r, j, cols, prev_ref):
        if j == 0:
            return jnp.concatenate([prev_ref[0, r], qkv_ref[0, r, 0:TILE, cols]], axis=0)
        return qkv_ref[0, r, (j - 1) * TILE:(j + 1) * TILE, cols]

    def scores(r, j):
        return _tile_scores(qkv_ref[0, r, j * TILE:(j + 1) * TILE, 0:GROUP_W], keys_values(r, j, k_cols, kp_ref))

    def finish(r, j, ss):
        o_acc, l_acc = _tile_finish(ss, keys_values(r, j, v_cols, vp_ref), band_first if j == 0 else band)
        if d == 1:
            rows = pl.ds(j * TILE, TILE)
        else:
            rows = pl.ds(j * TILE * d + r, TILE, stride=d)
        for half in range(GROUP_W // LANES):
            lanes = slice(half * LANES, (half + 1) * LANES)
            o_ref[0, half, rows, :] = o_acc[:, lanes]
            l_ref[0, half, rows, :] = l_acc[:, lanes]

    units = [(r, j) for r in range(d) for j in range(n_tiles)]
    ss_next = scores(*units[0])
    for n, unit in enumerate(units):
        ss = ss_next
        if n + 1 < len(units):
            ss_next = scores(*units[n + 1])
        finish(*unit, ss)


def _attn_prompt(qkv, g):
    b, d, tc, _ = qkv.shape
    t = tc * d
    sb = min(ATTN_SB, t)
    n_tiles = sb // d // TILE
    assert d == DILATIONS[g] and t % sb == 0 and sb % (d * TILE) == 0 and TILE == SPAN

    def prev_spec(col_block):
        return pl.BlockSpec((1, d, SPAN, GROUP_W),
                            lambda bi, s: (bi, 0, jnp.maximum(s * n_tiles - 1, 0), col_block))

    return pl.pallas_call(
        functools.partial(_attn_prompt_kernel, d=d, n_tiles=n_tiles),
        grid=(b, t // sb),
        in_specs=[pl.BlockSpec((1, d, sb // d, QKV_W), lambda bi, s: (bi, 0, s, 0)),
                  prev_spec(1), prev_spec(2)],
        out_specs=[pl.BlockSpec((1, 2, sb, LANES), lambda bi, s: (bi, 0, s, 0))] * 2,
        out_shape=[jax.ShapeDtypeStruct((b, 2, t, LANES), F32)] * 2,
        compiler_params=pltpu.CompilerParams(
            dimension_semantics=("arbitrary", "arbitrary"), vmem_limit_bytes=VMEM_LIMIT),
        name=f"attn_prompt_g{g}",
    )(qkv, qkv, qkv)


def _new_rows_transposed(new_ref, bb, g, steps):
    new = new_ref[bb, :, g * GROUP_W:(g + 1) * GROUP_W]
    return jnp.concatenate([new, jnp.zeros((LANES - steps, GROUP_W), F32)], axis=0).T


def _shift_window(bb, c_ref, n_ref, knt, vnt, steps):
    w = c_ref.shape[2]
    shift = LANES - steps
    lane_kv = lax.broadcasted_iota(jnp.int32, (KV_W, LANES), 1)
    nxt = pltpu.roll(c_ref[bb, :, 0:LANES], shift, 1)
    for cb in range(w // LANES):
        cur = nxt
        if cb + 1 < w // LANES:
            nxt = pltpu.roll(c_ref[bb, :, (cb + 1) * LANES:(cb + 2) * LANES], shift, 1)
        else:
            nxt = pltpu.roll(jnp.concatenate([knt, vnt], axis=0), shift, 1)
        n_ref[bb, :, cb * LANES:(cb + 1) * LANES] = jnp.where(lane_kv < shift, cur, nxt)


def _sample_scores(bb, q_ref, kn_ref, vn_ref, c_refs, n_refs, steps):
    rows = HEADS * steps
    step_of = lambda shape: lax.broadcasted_iota(jnp.int32, shape, 0) & (steps - 1)
    parts = []
    for g, c_ref in enumerate(c_refs):
        w = c_ref.shape[2]
        d = DILATIONS[g]
        knt = _new_rows_transposed(kn_ref, bb, g, steps)
        vnt = _new_rows_transposed(vn_ref, bb, g, steps)
        if g < len(n_refs):
            _shift_window(bb, c_ref, n_refs[g], knt, vnt, steps)

        qg = q_ref[bb, :, g * GROUP_W:(g + 1) * GROUP_W]
        qs = jnp.concatenate([qg * _head_mask(h, F32) for h in range(HEADS)], axis=0).astype(BF16)
        ck = c_ref[bb, 0:GROUP_W, :].astype(BF16)
        s_c = jnp.dot(qs, ck, preferred_element_type=F32)
        s_n = jnp.dot(qs, knt.astype(BF16), preferred_element_type=F32)
        t_c = step_of((rows, w))
        i_c = lax.broadcasted_iota(jnp.int32, (rows, w), 1)
        valid_c = (i_c >= t_c) & (((i_c - t_c) & (d - 1)) == 0)
        t_n = step_of((rows, LANES))
        i_n = lax.broadcasted_iota(jnp.int32, (rows, LANES), 1)
        valid_n = (i_n <= t_n) & (((t_n - i_n) & (d - 1)) == 0)
        s_c = jnp.where(valid_c, s_c, -jnp.inf)
        s_n = jnp.where(valid_n, s_n, -jnp.inf)
        m = jnp.maximum(jnp.max(s_c, axis=-1, keepdims=True), jnp.max(s_n, axis=-1, keepdims=True))
        p_c = jnp.exp(s_c - m)
        p_n = jnp.exp(s_n - m)
        z = jnp.sum(p_c, axis=-1, keepdims=True) + jnp.sum(p_n, axis=-1, keepdims=True)
        parts.append((p_c.astype(BF16), p_n.astype(BF16), z, m, vnt.astype(BF16)))
    return parts


def _sample_values(bb, parts, c_refs, o_ref, steps):
    lane256 = lax.broadcasted_iota(jnp.int32, (steps, GROUP_W), 1)
    outs, lses = [], []
    for (p_c, p_n, z, m, vnt), c_ref in zip(parts, c_refs):
        cv = c_ref[bb, GROUP_W:KV_W, :].astype(BF16)
        of = lax.dot_general(p_c, cv, NT_DIMS, preferred_element_type=F32)
        of = of + lax.dot_general(p_n, vnt, NT_DIMS, preferred_element_type=F32)
        outs.append(of / z)
        lses.append(m + jnp.log(z))
    top = jnp.maximum(jnp.maximum(lses[0], lses[1]), lses[2])
    es = [jnp.exp(l - top) for l in lses]
    den = es[0] + es[1] + es[2]
    for g in range(N_GROUPS):
        og = outs[g] * (es[g] / den)
        res = jnp.zeros((steps, GROUP_W), F32)
        for h in range(HEADS):
            res = jnp.where(_in_head(lane256, h), og[h * steps:(h + 1) * steps, :], res)
        o_ref[bb, :, g * GROUP_W:(g + 1) * GROUP_W] = res


def _post_kernel(*refs, tm, combine):
    if combine:
        (x_ref, o0_ref, o1_ref, o2_ref, l0_ref, l1_ref, l2_ref, co_ref,
         wo_ref, g2_ref, wu_ref, wd_ref, gf_ref, kn_ref, vn_ref, c2_ref, y_ref, n2_ref) = refs
        steps = kn_ref.shape[1]
        for bb in range(c2_ref.shape[0]):
            _shift_window(bb, c2_ref, n2_ref, _new_rows_transposed(kn_ref, bb, N_GROUPS - 1, steps),
                          _new_rows_transposed(vn_ref, bb, N_GROUPS - 1, steps), steps)
    else:
        x_ref, o_ref, co_ref, wo_ref, g2_ref, wu_ref, wd_ref, gf_ref, y_ref = refs
    halves = [pl.ds(k * (tm // 2), tm // 2) for k in range(2)]

    def mixer_input(rows):
        if not combine:
            return jnp.concatenate([o_ref[0, rows, :].astype(BF16), co_ref[0, rows, :]], axis=-1)
        slabs = lambda ref: jnp.concatenate([ref[0, 0, rows, :], ref[0, 1, rows, :]], axis=1)
        ls = [slabs(l_ref) for l_ref in (l0_ref, l1_ref, l2_ref)]
        top = jnp.maximum(jnp.maximum(ls[0], ls[1]), ls[2])
        es = [jnp.exp(l - top) for l in ls]
        den = es[0] + es[1] + es[2]
        parts = [(slabs(o_ref) * (e / den)).astype(BF16) for o_ref, e in zip((o0_ref, o1_ref, o2_ref), es)]
        return jnp.concatenate(parts + [co_ref[0, rows, :]], axis=-1)

    chunk = 1024
    n_chunks = wu_ref.shape[1] // chunk

    def mlp(xn):
        def up(c):
            hid = jnp.dot(xn, wu_ref[:, c * chunk:(c + 1) * chunk], preferred_element_type=F32)
            return jnp.square(jnp.maximum(hid, 0.0)).astype(BF16)

        acc = None
        hid_next = up(0)
        for c in range(n_chunks):
            hid = hid_next
            if c + 1 < n_chunks:
                hid_next = up(c + 1)
            part = jnp.dot(hid, wd_ref[c * chunk:(c + 1) * chunk, :], preferred_element_type=F32)
            acc = part if acc is None else acc + part
        return acc

    mixed = [mixer_input(rows) for rows in halves]
    x1 = [x_ref[0, rows, :] + jnp.dot(m, wo_ref[...], preferred_element_type=F32)
          for rows, m in zip(halves, mixed)]
    xn = [_rmsnorm(v, g2_ref[...]).astype(BF16) for v in x1]
    for rows, v, n in zip(halves, x1, xn):
        y_ref[0, rows, :] = _rmsnorm(v + mlp(n), gf_ref[...])


def _post(x, attn_inputs, co, w_out, g2, w_up, w_down, gf, combine, sample_shift=()):
    b, t, d_model = x.shape
    tm = min(t, POST_TM)
    assert t % tm == 0 and bool(sample_shift) == combine
    nt = t // tm

    def tok(a):
        if a.ndim == 4:
            return pl.BlockSpec((1, a.shape[1], tm, a.shape[3]), lambda bi, i: (bi, 0, i, 0))
        return pl.BlockSpec((1, tm, a.shape[2]), lambda bi, i: (bi, i, 0))

    const = lambda a: pl.BlockSpec(a.shape, lambda bi, i: (0, 0))
    out_specs, out_shape, shift_specs = [tok(x)], [jax.ShapeDtypeStruct((b, t, d_model), F32)], []
    if sample_shift:
        kn_s, vn_s, cache = sample_shift
        nb = cache.shape[0]
        bps = nb // (b * nt)
        assert bps * b * nt == nb
        per_step = lambda a: pl.BlockSpec((bps,) + a.shape[1:], lambda bi, i: (bi * nt + i, 0, 0))
        shift_specs = [per_step(kn_s), per_step(vn_s), per_step(cache)]
        out_specs.append(per_step(cache))
        out_shape.append(jax.ShapeDtypeStruct(cache.shape, F32))
    return pl.pallas_call(
        functools.partial(_post_kernel, tm=tm, combine=combine),
        grid=(b, nt),
        in_specs=[tok(x)] + [tok(a) for a in attn_inputs]
                 + [tok(co), const(w_out), const(g2), const(w_up), const(w_down), const(gf)] + shift_specs,
        out_specs=out_specs,
        out_shape=out_shape,
        compiler_params=pltpu.CompilerParams(
            dimension_semantics=("arbitrary", "arbitrary"), vmem_limit_bytes=VMEM_LIMIT),
        name="post_prompt" if combine else "post_sample",
    )(x, *attn_inputs, co, w_out, g2, w_up, w_down, gf, *sample_shift)


def kernel(x_prompt, x_sample, cache_kv_w128, cache_kv_w512, cache_kv_w2048, state_conv, norm_attn_g, w_in, conv_w, w_out, norm_mlp_g, w_up, w_down, norm_final_g):
    b, t, d_model = x_prompt.shape
    nb, steps, _ = x_sample.shape
    n_s = nb * steps
    assert w_in.shape[0] == 1, "single trunk layer"
    g1 = norm_attn_g[0][None, :]
    g2 = norm_mlp_g[0][None, :]
    gf = norm_final_g[None, :]
    w_in_b = w_in[0].astype(BF16)
    w_out_b = w_out[0].astype(BF16)
    w_up_b = w_up[0].astype(BF16)
    w_down_b = w_down[0].astype(BF16)
    cw = conv_w[0]

    caches = [jnp.transpose(c[0], (0, 2, 3, 4, 1)).reshape(nb, KV_W, w)
              for c, w in zip((cache_kv_w128, cache_kv_w512, cache_kv_w2048), WINDOWS)]
    state_pad = jnp.pad(state_conv[0], ((0, 0), (0, steps - (CONV_K - 1)), (0, 0))).reshape(n_s, CONV_CH)
    q_s, kn, vn, co_s, u_s = _proj_sample(x_sample.reshape(n_s, d_model), g1, w_in_b, cw, state_pad)
    r3 = lambda a: a.reshape(nb, steps, ATTN_W)
    o0, l0, o1, l1, qkv2, co, kv0, kv1, kv2, ctail, o_s, n0, n1 = _proj_prompt(
        x_prompt, g1, w_in_b, cw, r3(q_s), r3(kn), r3(vn), caches)
    (y_s,) = _post(x_sample.reshape(1, n_s, d_model), [o_s.reshape(1, n_s, ATTN_W)], co_s.reshape(1, n_s, CONV_CH),
                w_out_b, g2, w_up_b, w_down_b, gf, False)

    attn = [(o0, l0), (o1, l1), _attn_prompt(qkv2, 2)]
    attn_inputs = [o for o, _ in attn] + [l for _, l in attn]
    y_p, n2 = _post(x_prompt, attn_inputs, co, w_out_b, g2, w_up_b, w_down_b, gf, True,
                    sample_shift=(r3(kn), r3(vn), caches[2]))

    kv_p = lambda a, w: a.reshape(1, b, w, 2, HEADS, HEAD_DIM)
    kv_s = lambda a, w: jnp.transpose(a.reshape(nb, 2, HEADS, HEAD_DIM, w), (0, 4, 1, 2, 3))[None]
    return (y_p,
            y_s.reshape(nb, steps, d_model),
            kv_p(kv0, WINDOWS[0]),
            kv_p(kv1, WINDOWS[1]),
            kv_p(kv2, WINDOWS[2]),
            ctail[:, SUBLANES - (CONV_K - 1):, :][None],
            kv_s(n0, WINDOWS[0]),
            kv_s(n1, WINDOWS[1]),
            kv_s(n2, WINDOWS[2]),
            u_s.reshape(nb, steps, CONV_CH)[:, steps - (CONV_K - 1):, :][None])
```

```python
import functools

import jax
import jax.numpy as jnp
from jax import lax
from jax.experimental import pallas as pl
from jax.experimental.pallas import tpu as pltpu

F32 = jnp.float32
BF16 = jnp.bfloat16

EPS = 1e-6
N_GROUPS = 3
HEADS = 4
HEAD_DIM = 64
GROUP_W = HEADS * HEAD_DIM
ATTN_W = N_GROUPS * GROUP_W
QKV_W = 3 * GROUP_W
CONV_CH = 256
CONV_K = 3
KV_W = 2 * GROUP_W
WINDOWS = (128, 512, 2048)
DILATIONS = (1, 4, 16)
SPAN = 128
QK_SCALE = HEAD_DIM ** -0.5
LOG2E = 1.4426950408889634
LN2 = 0.6931471805599453

_Q0, _K0, _V0, _B0, _C0, _H0, _END = 0, 768, 1536, 2304, 2560, 2816, 3072

SUBLANES = 8
LANES = 128
VMEM_LIMIT = 56 * 1024 * 1024

PROJ_TM = 512
POST_TM = 512
ATTN_SB = 4096
TILE = 128
NT_DIMS = (((1,), (1,)), ((), ()))


def _rmsnorm(x, g):
    y = x * lax.rsqrt(jnp.mean(x * x, axis=-1, keepdims=True) + EPS)
    return y * g


def _head_mask(h, dtype):
    lane = lax.broadcasted_iota(jnp.int32, (1, GROUP_W), 1)
    return ((lane >= h * HEAD_DIM) & (lane < (h + 1) * HEAD_DIM)).astype(F32).astype(dtype)


def _in_head(lane, h):
    return (lane >= h * HEAD_DIM) & (lane < (h + 1) * HEAD_DIM)


def _gated_conv(gate_b, u, prev1, prev2, cw_ref):
    y = cw_ref[0:1, :] * prev2
    y = y + cw_ref[1:2, :] * prev1
    y = y + cw_ref[2:3, :] * u
    return gate_b * y


def _band_masks(first):
    key = lax.broadcasted_iota(jnp.int32, (TILE + SPAN, TILE), 0)
    qry = lax.broadcasted_iota(jnp.int32, (TILE + SPAN, TILE), 1)
    band = (key >= qry) & (key <= qry + SPAN)
    return band, band & (key + jnp.where(first, 0, SPAN) >= SPAN)


def _tile_scores(q, kk):
    ss = []
    for h in range(0, HEADS, 2):
        q2 = jnp.concatenate([q * _head_mask(h, BF16), q * _head_mask(h + 1, BF16)], axis=0)
        s2 = lax.dot_general(kk, q2, NT_DIMS, preferred_element_type=F32)
        ss += [s2[:, 0:TILE], s2[:, TILE:2 * TILE]]
    return ss


def _tile_finish(ss, vv, valid):
    vt = vv.T
    o_t, l_t = [], []
    for h in range(HEADS):
        s = jnp.where(valid, ss[h], -jnp.inf)
        m = jnp.max(s, axis=0, keepdims=True)
        p = jnp.exp2(s - m)
        z = jnp.sum(p, axis=0, keepdims=True)
        acc = jnp.dot(vt[h * HEAD_DIM:(h + 1) * HEAD_DIM, :], p.astype(BF16),
                      preferred_element_type=F32)
        o_t.append(acc * (1.0 / z))
        l_t.append(jnp.broadcast_to(m * LN2 + jnp.log(z), (HEAD_DIM, TILE)))
    return jnp.concatenate(o_t, axis=0).T, jnp.concatenate(l_t, axis=0).T


def _proj_prompt_kernel(x_ref, g_ref, w_ref, cw_ref, qs_ref, kn_ref, vn_ref, c0_ref, c1_ref, c2_ref,
                        o0_ref, l0_ref, o1_ref, l1_ref, qkv2_ref, co_ref, kv0_ref, kv1_ref, kv2_ref, ct_ref,
                        os_ref, n0_ref, n1_ref,
                        carry_ref, slab_ref, kprev_ref, vprev_ref, k1prev_ref, v1prev_ref, *, tm, steps):
    i = pl.program_id(1)

    @pl.when(i == 0)
    def _():
        carry_ref[...] = jnp.zeros_like(carry_ref)

    @pl.when((i == 0) & (pl.program_id(0) == 0))
    def _():
        kprev_ref[...] = jnp.zeros_like(kprev_ref)
        vprev_ref[...] = jnp.zeros_like(vprev_ref)
        k1prev_ref[...] = jnp.zeros_like(k1prev_ref)
        v1prev_ref[...] = jnp.zeros_like(v1prev_ref)

    c_refs = (c0_ref, c1_ref, c2_ref)
    sample = [_sample_scores(bb, qs_ref, kn_ref, vn_ref, c_refs, (n0_ref, n1_ref), steps)
              for bb in range(qs_ref.shape[0])]

    xn = _rmsnorm(x_ref[0], g_ref[...]).astype(BF16)

    def proj(lo, hi):
        return jnp.dot(xn, w_ref[:, lo:hi], preferred_element_type=F32)

    qf = proj(_Q0, _K0) * (QK_SCALE * LOG2E)
    kf = proj(_K0, _V0)

    q0, k0 = (a[:, 0:GROUP_W].astype(BF16) for a in (qf, kf))
    slot = lax.rem(i, 2)
    band, band_first = _band_masks(i == 0)
    n_tiles = tm // TILE

    def keys_values(j, cur, prev_ref):
        if j == 0:
            return jnp.concatenate([prev_ref[slot], cur[0:TILE]], axis=0)
        return cur[(j - 1) * TILE:(j + 1) * TILE]

    scores = [_tile_scores(q0[j * TILE:(j + 1) * TILE], keys_values(j, k0, kprev_ref)) for j in range(n_tiles)]
    kprev_ref[1 - slot] = k0[tm - SPAN:tm]

    d1 = DILATIONS[1]
    g1_cols = slice(GROUP_W, 2 * GROUP_W)

    def regroup(x, part):
        for half in range(GROUP_W // LANES):
            slab_ref[0, 2 * part + half] = x[:, half * LANES:(half + 1) * LANES]
        return [jnp.concatenate([slab_ref[0, 2 * part + half, pl.ds(r, TILE, stride=d1), :]
                                 for half in range(GROUP_W // LANES)], axis=1).astype(BF16) for r in range(d1)]

    q1, k1 = regroup(qf[:, g1_cols], 0), regroup(kf[:, g1_cols], 1)
    scores1 = [_tile_scores(q1[r], jnp.concatenate([k1prev_ref[slot, r], k1[r]], axis=0)) for r in range(d1)]
    for r in range(d1):
        k1prev_ref[1 - slot, r] = k1[r]

    vf = proj(_V0, _B0)
    for g, (qkv_ref, kv_ref) in enumerate(((None, kv0_ref), (None, kv1_ref), (qkv2_ref, kv2_ref))):
        cols = slice(g * GROUP_W, (g + 1) * GROUP_W)
        rows = kv_ref.shape[1]
        kv_ref[0, :, 0:GROUP_W] = kf[tm - rows:tm, cols]
        kv_ref[0, :, GROUP_W:KV_W] = vf[tm - rows:tm, cols]
        d = DILATIONS[g]
        if qkv_ref is None:
            continue
        piece = jnp.concatenate([qf[:, cols], kf[:, cols], vf[:, cols]], axis=1)
        slabs = slab_ref.at[g - 1]
        for s in range(QKV_W // LANES):
            slabs[s] = piece[:, s * LANES:(s + 1) * LANES]
        for r in range(d):
            for s in range(QKV_W // LANES):
                qkv_ref[0, r, :, s * LANES:(s + 1) * LANES] = (
                    slabs[s, pl.ds(r, tm // d, stride=d), :].astype(BF16))

    for bb, parts in enumerate(sample):
        _sample_values(bb, parts, c_refs, os_ref, steps)

    v0 = vf[:, 0:GROUP_W].astype(BF16)
    for j in range(n_tiles):
        o_acc, l_acc = _tile_finish(scores[j], keys_values(j, v0, vprev_ref), band_first if j == 0 else band)
        for half in range(GROUP_W // LANES):
            lanes = slice(half * LANES, (half + 1) * LANES)
            o0_ref[0, half, j * TILE:(j + 1) * TILE, :] = o_acc[:, lanes]
            l0_ref[0, half, j * TILE:(j + 1) * TILE, :] = l_acc[:, lanes]
    vprev_ref[1 - slot] = v0[tm - SPAN:tm]

    v1 = regroup(vf[:, g1_cols], 2)
    for r in range(d1):
        o_acc, l_acc = _tile_finish(scores1[r], jnp.concatenate([v1prev_ref[slot, r], v1[r]], axis=0), band_first)
        for half in range(GROUP_W // LANES):
            lanes = slice(half * LANES, (half + 1) * LANES)
            o1_ref[0, half, pl.ds(r, TILE, stride=d1), :] = o_acc[:, lanes]
            l1_ref[0, half, pl.ds(r, TILE, stride=d1), :] = l_acc[:, lanes]
        v1prev_ref[1 - slot, r] = v1[r]

    gate_b = proj(_B0, _C0)
    u = proj(_C0, _H0) * proj(_H0, _END)
    row = lax.broadcasted_iota(jnp.int32, (tm, CONV_CH), 0)
    carry = carry_ref[...]
    prev1 = jnp.where(row == 0, carry[7:8, :], pltpu.roll(u, 1, 0))
    prev2 = jnp.where(row == 0, carry[6:7, :],
                      jnp.where(row == 1, carry[7:8, :], pltpu.roll(u, 2, 0)))
    co_ref[0] = _gated_conv(gate_b, u, prev1, prev2, cw_ref).astype(BF16)
    tail = u[tm - SUBLANES:tm, :]
    carry_ref[...] = tail
    ct_ref[0] = tail


def _proj_prompt(x, g, w_in, conv_w, q_s, kn_s, vn_s, caches):
    b, t, d_model = x.shape
    tm = PROJ_TM
    nt = t // tm
    assert t % tm == 0 and t >= WINDOWS[-1]
    nb, steps, _ = q_s.shape
    bps = nb // (b * nt)
    assert bps * b * nt == nb and steps == SUBLANES
    assert all(c.shape[1:] == (KV_W, w) for c, w in zip(caches, WINDOWS))
    stok = pl.BlockSpec((bps, steps, ATTN_W), lambda bi, i: (bi * nt + i, 0, 0))
    cspecs = [pl.BlockSpec((bps, KV_W, w), lambda bi, i: (bi * nt + i, 0, 0)) for w in WINDOWS]

    def tail_spec(w):
        rows = min(w, tm)
        nblk = max(w // tm, 1)
        return (pl.BlockSpec((1, rows, KV_W), lambda bi, i: (bi, jnp.maximum(i - (nt - nblk), 0), 0)),
                jax.ShapeDtypeStruct((b, w, KV_W), F32))

    tails = [tail_spec(w) for w in WINDOWS]
    tok = lambda width: pl.BlockSpec((1, tm, width), lambda bi, i: (bi, i, 0))
    const = lambda shape: pl.BlockSpec(shape, lambda bi, i: (0, 0))
    qkv_specs = [pl.BlockSpec((1, 2, tm, LANES), lambda bi, i: (bi, 0, i, 0))] * 4 + [
        pl.BlockSpec((1, d, tm // d, QKV_W), lambda bi, i: (bi, 0, i, 0)) for d in DILATIONS[2:]]
    qkv_shapes = [jax.ShapeDtypeStruct((b, 2, t, LANES), F32)] * 4 + [
        jax.ShapeDtypeStruct((b, d, t // d, QKV_W), BF16) for d in DILATIONS[2:]]
    assert DILATIONS[0] == 1 and tm % TILE == 0 and TILE == SPAN and tm // DILATIONS[1] == TILE
    return pl.pallas_call(
        functools.partial(_proj_prompt_kernel, tm=tm, steps=steps),
        grid=(b, nt),
        in_specs=[tok(d_model), const((1, d_model)), const((d_model, _END)), const((CONV_K, CONV_CH)),
                  stok, stok, stok] + cspecs,
        out_specs=qkv_specs + [tok(CONV_CH)] + [s for s, _ in tails]
                  + [pl.BlockSpec((1, SUBLANES, CONV_CH), lambda bi, i: (bi, 0, 0)), stok] + cspecs[:2],
        out_shape=qkv_shapes + [jax.ShapeDtypeStruct((b, t, CONV_CH), BF16)] + [s for _, s in tails]
                  + [jax.ShapeDtypeStruct((b, SUBLANES, CONV_CH), F32),
                     jax.ShapeDtypeStruct((nb, steps, ATTN_W), F32)]
                  + [jax.ShapeDtypeStruct((nb, KV_W, w), F32) for w in WINDOWS[:2]],
        scratch_shapes=[pltpu.VMEM((SUBLANES, CONV_CH), F32),
                        pltpu.VMEM((N_GROUPS - 1, QKV_W // LANES, tm, LANES), F32),
                        pltpu.VMEM((2, SPAN, GROUP_W), BF16), pltpu.VMEM((2, SPAN, GROUP_W), BF16),
                        pltpu.VMEM((2, DILATIONS[1], SPAN, GROUP_W), BF16),
                        pltpu.VMEM((2, DILATIONS[1], SPAN, GROUP_W), BF16)],
        compiler_params=pltpu.CompilerParams(
            dimension_semantics=("arbitrary", "arbitrary"), vmem_limit_bytes=VMEM_LIMIT),
        name="proj_prompt",
    )(x, g, w_in, conv_w, q_s, kn_s, vn_s, *caches)


def _proj_sample_kernel(x_ref, g_ref, w_ref, cw_ref, sp_ref,
                        q_ref, kn_ref, vn_ref, co_ref, u_ref, *, tm):
    xn = _rmsnorm(x_ref[...], g_ref[...]).astype(BF16)

    def proj(lo, hi):
        return jnp.dot(xn, w_ref[:, lo:hi], preferred_element_type=F32)

    q_ref[...] = proj(_Q0, _K0) * QK_SCALE
    kn_ref[...] = proj(_K0, _V0)
    vn_ref[...] = proj(_V0, _B0)
    gate_b = proj(_B0, _C0)
    u = proj(_C0, _H0) * proj(_H0, _END)
    step = lax.broadcasted_iota(jnp.int32, (tm, CONV_CH), 0) & (SUBLANES - 1)
    sp = sp_ref[...]
    prev1 = jnp.where(step == 0, pltpu.roll(sp, tm - 1, 0), pltpu.roll(u, 1, 0))
    prev2 = jnp.where(step < 2, sp, pltpu.roll(u, 2, 0))
    co_ref[...] = _gated_conv(gate_b, u, prev1, prev2, cw_ref).astype(BF16)
    u_ref[...] = u


def _proj_sample(x, g, w_in, conv_w, state_pad):
    n, d_model = x.shape
    tm = min(n, POST_TM)
    assert n % tm == 0 and tm % SUBLANES == 0
    tok = lambda width: pl.BlockSpec((tm, width), lambda i: (i, 0))
    const = lambda shape: pl.BlockSpec(shape, lambda i: (0, 0))
    return pl.pallas_call(
        functools.partial(_proj_sample_kernel, tm=tm),
        grid=(n // tm,),
        in_specs=[tok(d_model), const((1, d_model)), const((d_model, _END)), const((CONV_K, CONV_CH)),
                  tok(CONV_CH)],
        out_specs=[tok(ATTN_W), tok(ATTN_W), tok(ATTN_W), tok(CONV_CH), tok(CONV_CH)],
        out_shape=[jax.ShapeDtypeStruct((n, ATTN_W), F32)] * 3
                  + [jax.ShapeDtypeStruct((n, CONV_CH), BF16), jax.ShapeDtypeStruct((n, CONV_CH), F32)],
        compiler_params=pltpu.CompilerParams(
            dimension_semantics=("arbitrary",), vmem_limit_bytes=VMEM_LIMIT),
        name="proj_sample",
    )(x, g, w_in, conv_w, state_pad)


def _attn_prompt_kernel(qkv_ref, kp_ref, vp_ref, o_ref, l_ref, *, d, n_tiles):
    band, band_first = _band_masks(pl.program_id(1) == 0)
    k_cols = slice(GROUP_W, 2 * GROUP_W)
    v_cols = slice(2 * GROUP_W, 3 * GROUP_W)

    def keys_values(r, j, cols, prev_ref):
        if j == 0:
            return jnp.concatenate([prev_ref[0, r], qkv_ref[0, r, 0:TILE, cols]], axis=0)
        return qkv_ref[0, r, (j - 1) * TILE:(j + 1) * TILE, cols]

    def scores(r, j):
        return _tile_scores(qkv_ref[0, r, j * TILE:(j + 1) * TILE, 0:GROUP_W], keys_values(r, j, k_cols, kp_ref))

    def finish(r, j, ss):
        o_acc, l_acc = _tile_finish(ss, keys_values(r, j, v_cols, vp_ref), band_first if j == 0 else band)
        if d == 1:
            rows = pl.ds(j * TILE, TILE)
        else:
            rows = pl.ds(j * TILE * d + r, TILE, stride=d)
        for half in range(GROUP_W // LANES):
            lanes = slice(half * LANES, (half + 1) * LANES)
            o_ref[0, half, rows, :] = o_acc[:, lanes]
            l_ref[0, half, rows, :] = l_acc[:, lanes]

    units = [(r, j) for r in range(d) for j in range(n_tiles)]
    ss_next = scores(*units[0])
    for n, unit in enumerate(units):
        ss = ss_next
        if n + 1 < len(units):
            ss_next = scores(*units[n + 1])
        finish(*unit, ss)


def _attn_prompt(qkv, g):
    b, d, tc, _ = qkv.shape
    t = tc * d
    sb = min(ATTN_SB, t)
    n_tiles = sb // d // TILE
    assert d == DILATIONS[g] and t % sb == 0 and sb % (d * TILE) == 0 and TILE == SPAN

    def prev_spec(col_block):
        return pl.BlockSpec((1, d, SPAN, GROUP_W),
                            lambda bi, s: (bi, 0, jnp.maximum(s * n_tiles - 1, 0), col_block))

    return pl.pallas_call(
        functools.partial(_attn_prompt_kernel, d=d, n_tiles=n_tiles),
        grid=(b, t // sb),
        in_specs=[pl.BlockSpec((1, d, sb // d, QKV_W), lambda bi, s: (bi, 0, s, 0)),
                  prev_spec(1), prev_spec(2)],
        out_specs=[pl.BlockSpec((1, 2, sb, LANES), lambda bi, s: (bi, 0, s, 0))] * 2,
        out_shape=[jax.ShapeDtypeStruct((b, 2, t, LANES), F32)] * 2,
        compiler_params=pltpu.CompilerParams(
            dimension_semantics=("arbitrary", "arbitrary"), vmem_limit_bytes=VMEM_LIMIT),
        name=f"attn_prompt_g{g}",
    )(qkv, qkv, qkv)


def _new_rows_transposed(new_ref, bb, g, steps):
    new = new_ref[bb, :, g * GROUP_W:(g + 1) * GROUP_W]
    return jnp.concatenate([new, jnp.zeros((LANES - steps, GROUP_W), F32)], axis=0).T


def _shift_window(bb, c_ref, n_ref, knt, vnt, steps):
    w = c_ref.shape[2]
    shift = LANES - steps
    lane_kv = lax.broadcasted_iota(jnp.int32, (KV_W, LANES), 1)
    nxt = pltpu.roll(c_ref[bb, :, 0:LANES], shift, 1)
    for cb in range(w // LANES):
        cur = nxt
        if cb + 1 < w // LANES:
            nxt = pltpu.roll(c_ref[bb, :, (cb + 1) * LANES:(cb + 2) * LANES], shift, 1)
        else:
            nxt = pltpu.roll(jnp.concatenate([knt, vnt], axis=0), shift, 1)
        n_ref[bb, :, cb * LANES:(cb + 1) * LANES] = jnp.where(lane_kv < shift, cur, nxt)


def _sample_scores(bb, q_ref, kn_ref, vn_ref, c_refs, n_refs, steps):
    rows = HEADS * steps
    step_of = lambda shape: lax.broadcasted_iota(jnp.int32, shape, 0) & (steps - 1)
    parts = []
    for g, c_ref in enumerate(c_refs):
        w = c_ref.shape[2]
        d = DILATIONS[g]
        knt = _new_rows_transposed(kn_ref, bb, g, steps)
        vnt = _new_rows_transposed(vn_ref, bb, g, steps)
        if g < len(n_refs):
            _shift_window(bb, c_ref, n_refs[g], knt, vnt, steps)

        qg = q_ref[bb, :, g * GROUP_W:(g + 1) * GROUP_W]
        qs = jnp.concatenate([qg * _head_mask(h, F32) for h in range(HEADS)], axis=0).astype(BF16)
        ck = c_ref[bb, 0:GROUP_W, :].astype(BF16)
        s_c = jnp.dot(qs, ck, preferred_element_type=F32)
        s_n = jnp.dot(qs, knt.astype(BF16), preferred_element_type=F32)
        t_c = step_of((rows, w))
        i_c = lax.broadcasted_iota(jnp.int32, (rows, w), 1)
        valid_c = (i_c >= t_c) & (((i_c - t_c) & (d - 1)) == 0)
        t_n = step_of((rows, LANES))
        i_n = lax.broadcasted_iota(jnp.int32, (rows, LANES), 1)
        valid_n = (i_n <= t_n) & (((t_n - i_n) & (d - 1)) == 0)
        s_c = jnp.where(valid_c, s_c, -jnp.inf)
        s_n = jnp.where(valid_n, s_n, -jnp.inf)
        m = jnp.maximum(jnp.max(s_c, axis=-1, keepdims=True), jnp.max(s_n, axis=-1, keepdims=True))
        p_c = jnp.exp(s_c - m)
        p_n = jnp.exp(s_n - m)
        z = jnp.sum(p_c, axis=-1, keepdims=True) + jnp.sum(p_n, axis=-1, keepdims=True)
        parts.append((p_c.astype(BF16), p_n.astype(BF16), z, m, vnt.astype(BF16)))
    return parts


def _sample_values(bb, parts, c_refs, o_ref, steps):
    lane256 = lax.broadcasted_iota(jnp.int32, (steps, GROUP_W), 1)
    outs, lses = [], []
    for (p_c, p_n, z, m, vnt), c_ref in zip(parts, c_refs):
        cv = c_ref[bb, GROUP_W:KV_W, :].astype(BF16)
        of = lax.dot_general(p_c, cv, NT_DIMS, preferred_element_type=F32)
        of = of + lax.dot_general(p_n, vnt, NT_DIMS, preferred_element_type=F32)
        outs.append(of / z)
        lses.append(m + jnp.log(z))
    top = jnp.maximum(jnp.maximum(lses[0], lses[1]), lses[2])
    es = [jnp.exp(l - top) for l in lses]
    den = es[0] + es[1] + es[2]
    for g in range(N_GROUPS):
        og = outs[g] * (es[g] / den)
        res = jnp.zeros((steps, GROUP_W), F32)
        for h in range(HEADS):
            res = jnp.where(_in_head(lane256, h), og[h * steps:(h + 1) * steps, :], res)
        o_ref[bb, :, g * GROUP_W:(g + 1) * GROUP_W] = res


def _post_kernel(*refs, tm, combine):
    if combine:
        (x_ref, o0_ref, o1_ref, o2_ref, l0_ref, l1_ref, l2_ref, co_ref,
         wo_ref, g2_ref, wu_ref, wd_ref, gf_ref, kn_ref, vn_ref, c2_ref, y_ref, n2_ref) = refs
        steps = kn_ref.shape[1]
        for bb in range(c2_ref.shape[0]):
            _shift_window(bb, c2_ref, n2_ref, _new_rows_transposed(kn_ref, bb, N_GROUPS - 1, steps),
                          _new_rows_transposed(vn_ref, bb, N_GROUPS - 1, steps), steps)
    else:
        x_ref, o_ref, co_ref, wo_ref, g2_ref, wu_ref, wd_ref, gf_ref, y_ref = refs
    halves = [pl.ds(k * (tm // 2), tm // 2) for k in range(2)]

    def mixer_input(rows):
        if not combine:
            return jnp.concatenate([o_ref[0, rows, :].astype(BF16), co_ref[0, rows, :]], axis=-1)
        slabs = lambda ref: jnp.concatenate([ref[0, 0, rows, :], ref[0, 1, rows, :]], axis=1)
        ls = [slabs(l_ref) for l_ref in (l0_ref, l1_ref, l2_ref)]
        top = jnp.maximum(jnp.maximum(ls[0], ls[1]), ls[2])
        es = [jnp.exp(l - top) for l in ls]
        den = es[0] + es[1] + es[2]
        parts = [(slabs(o_ref) * (e / den)).astype(BF16) for o_ref, e in zip((o0_ref, o1_ref, o2_ref), es)]
        return jnp.concatenate(parts + [co_ref[0, rows, :]], axis=-1)

    chunk = 1024
    n_chunks = wu_ref.shape[1] // chunk

    def mlp(xn):
        def up(c):
            hid = jnp.dot(xn, wu_ref[:, c * chunk:(c + 1) * chunk], preferred_element_type=F32)
            return jnp.square(jnp.maximum(hid, 0.0)).astype(BF16)

        acc = None
        hid_next = up(0)
        for c in range(n_chunks):
            hid = hid_next
            if c + 1 < n_chunks:
                hid_next = up(c + 1)
            part = jnp.dot(hid, wd_ref[c * chunk:(c + 1) * chunk, :], preferred_element_type=F32)
            acc = part if acc is None else acc + part
        return acc

    mixed = [mixer_input(rows) for rows in halves]
    x1 = [x_ref[0, rows, :] + jnp.dot(m, wo_ref[...], preferred_element_type=F32)
          for rows, m in zip(halves, mixed)]
    xn = [_rmsnorm(v, g2_ref[...]).astype(BF16) for v in x1]
    for rows, v, n in zip(halves, x1, xn):
        y_ref[0, rows, :] = _rmsnorm(v + mlp(n), gf_ref[...])


def _post(x, attn_inputs, co, w_out, g2, w_up, w_down, gf, combine, sample_shift=()):
    b, t, d_model = x.shape
    tm = min(t, POST_TM)
    assert t % tm == 0 and bool(sample_shift) == combine
    nt = t // tm

    def tok(a):
        if a.ndim == 4:
            return pl.BlockSpec((1, a.shape[1], tm, a.shape[3]), lambda bi, i: (bi, 0, i, 0))
        return pl.BlockSpec((1, tm, a.shape[2]), lambda bi, i: (bi, i, 0))

    const = lambda a: pl.BlockSpec(a.shape, lambda bi, i: (0, 0))
    out_specs, out_shape, shift_specs = [tok(x)], [jax.ShapeDtypeStruct((b, t, d_model), F32)], []
    if sample_shift:
        kn_s, vn_s, cache = sample_shift
        nb = cache.shape[0]
        bps = nb // (b * nt)
        assert bps * b * nt == nb
        per_step = lambda a: pl.BlockSpec((bps,) + a.shape[1:], lambda bi, i: (bi * nt + i, 0, 0))
        shift_specs = [per_step(kn_s), per_step(vn_s), per_step(cache)]
        out_specs.append(per_step(cache))
        out_shape.append(jax.ShapeDtypeStruct(cache.shape, F32))
    return pl.pallas_call(
        functools.partial(_post_kernel, tm=tm, combine=combine),
        grid=(b, nt),
        in_specs=[tok(x)] + [tok(a) for a in attn_inputs]
                 + [tok(co), const(w_out), const(g2), const(w_up), const(w_down), const(gf)] + shift_specs,
        out_specs=out_specs,
        out_shape=out_shape,
        compiler_params=pltpu.CompilerParams(
            dimension_semantics=("arbitrary", "arbitrary"), vmem_limit_bytes=VMEM_LIMIT),
        name="post_prompt" if combine else "post_sample",
    )(x, *attn_inputs, co, w_out, g2, w_up, w_down, gf, *sample_shift)


def kernel(x_prompt, x_sample, cache_kv_w128, cache_kv_w512, cache_kv_w2048, state_conv, norm_attn_g, w_in, conv_w, w_out, norm_mlp_g, w_up, w_down, norm_final_g):
    b, t, d_model = x_prompt.shape
    nb, steps, _ = x_sample.shape
    n_s = nb * steps
    assert w_in.shape[0] == 1, "single trunk layer"
    g1 = norm_attn_g[0][None, :]
    g2 = norm_mlp_g[0][None, :]
    gf = norm_final_g[None, :]
    w_in_b = w_in[0].astype(BF16)
    w_out_b = w_out[0].astype(BF16)
    w_up_b = w_up[0].astype(BF16)
    w_down_b = w_down[0].astype(BF16)
    cw = conv_w[0]

    caches = [jnp.transpose(c[0], (0, 2, 3, 4, 1)).reshape(nb, KV_W, w)
              for c, w in zip((cache_kv_w128, cache_kv_w512, cache_kv_w2048), WINDOWS)]
    state_pad = jnp.pad(state_conv[0], ((0, 0), (0, steps - (CONV_K - 1)), (0, 0))).reshape(n_s, CONV_CH)
    q_s, kn, vn, co_s, u_s = _proj_sample(x_sample.reshape(n_s, d_model), g1, w_in_b, cw, state_pad)
    r3 = lambda a: a.reshape(nb, steps, ATTN_W)
    o0, l0, o1, l1, qkv2, co, kv0, kv1, kv2, ctail, o_s, n0, n1 = _proj_prompt(
        x_prompt, g1, w_in_b, cw, r3(q_s), r3(kn), r3(vn), caches)
    (y_s,) = _post(x_sample.reshape(1, n_s, d_model), [o_s.reshape(1, n_s, ATTN_W)], co_s.reshape(1, n_s, CONV_CH),
                w_out_b, g2, w_up_b, w_down_b, gf, False)

    attn = [(o0, l0), (o1, l1), _attn_prompt(qkv2, 2)]
    attn_inputs = [o for o, _ in attn] + [l for _, l in attn]
    y_p, n2 = _post(x_prompt, attn_inputs, co, w_out_b, g2, w_up_b, w_down_b, gf, True,
                    sample_shift=(r3(kn), r3(vn), caches[2]))

    kv_p = lambda a, w: a.reshape(1, b, w, 2, HEADS, HEAD_DIM)
    kv_s = lambda a, w: jnp.transpose(a.reshape(nb, 2, HEADS, HEAD_DIM, w), (0, 4, 1, 2, 3))[None]
    return (y_p,
            y_s.reshape(nb, steps, d_model),
            kv_p(kv0, WINDOWS[0]),
            kv_p(kv1, WINDOWS[1]),
            kv_p(kv2, WINDOWS[2]),
            ctail[:, SUBLANES - (CONV_K - 1):, :][None],
            kv_s(n0, WINDOWS[0]),
            kv_s(n1, WINDOWS[1]),
            kv_s(n2, WINDOWS[2]),
            u_s.reshape(nb, steps, CONV_CH)[:, steps - (CONV_K - 1):, :][None])
```
